```python
import math
import jax, jax.numpy as jnp
from jax import lax
import numpy as np

D_MODEL = 1024
BATCH = 8
SEQ = 4096
DEPTH = 2

ATT_HEADS = 4
ATT_QK_DIM = 64
ATT_V_DIM = 2 * ATT_QK_DIM
ATT_WIDTH = ATT_HEADS * ATT_V_DIM
POOL_GROUPS = 4
POOL_WINDOWS = (2, 4, 8, 16)
POOL_WIDTH = 256
POOL_GDIM = POOL_WIDTH // POOL_GROUPS
CONV_WIDTH = 256
CONV_K = 3
MIX_WIDTH = ATT_WIDTH + POOL_WIDTH + CONV_WIDTH
QK_COLS = ATT_HEADS * 2 * ATT_QK_DIM
IN_COLS = 2 * QK_COLS + ATT_WIDTH + POOL_WIDTH + 3 * CONV_WIDTH
D_FF = ((8 * D_MODEL // 3 + 255) // 256) * 256
NUM_BUCKETS = 32
MAX_EXACT = NUM_BUCKETS // 2
MAX_DISTANCE = 128
Q_BLOCK = 128
EPS = 1e-6
SUBLN_EPS = 1e-5

kernel_name = "hybrid_diffattn_pool_shortconv_block"


def rmsnorm(x, g, eps=EPS):
    xf = x.astype(jnp.float32)
    y = xf * lax.rsqrt(jnp.mean(xf * xf, axis=-1, keepdims=True) + eps)
    return (y * g.astype(jnp.float32)).astype(x.dtype)


def rel_bucket(dist):
    n = jnp.maximum(dist, 0)
    nf = jnp.maximum(n, 1).astype(jnp.float32)
    large = MAX_EXACT + (jnp.log(nf / MAX_EXACT) / math.log(MAX_DISTANCE / MAX_EXACT)
                         * (NUM_BUCKETS - MAX_EXACT)).astype(jnp.int32)
    large = jnp.minimum(large, NUM_BUCKETS - 1)
    return jnp.where(n < MAX_EXACT, n, large)


def diff_attention(q1, q2, k1, k2, v, rel_bias, lam):
    b, h, s, _ = q1.shape
    nb = s // Q_BLOCK
    scale = ATT_QK_DIM ** -0.5
    kpos = jnp.arange(s)
    k1f, k2f, vf = k1.astype(jnp.float32), k2.astype(jnp.float32), v.astype(jnp.float32)

    def block(i):
        qs = i * Q_BLOCK
        qb1 = lax.dynamic_slice_in_dim(q1, qs, Q_BLOCK, axis=2).astype(jnp.float32)
        qb2 = lax.dynamic_slice_in_dim(q2, qs, Q_BLOCK, axis=2).astype(jnp.float32)
        dist = (qs + jnp.arange(Q_BLOCK))[:, None] - kpos[None, :]
        bias = jnp.transpose(rel_bias.astype(jnp.float32)[rel_bucket(dist)], (2, 0, 1))
        mask = dist >= 0
        s1 = jnp.einsum('bhqd,bhkd->bhqk', qb1, k1f) * scale + bias
        s2 = jnp.einsum('bhqd,bhkd->bhqk', qb2, k2f) * scale + bias
        p1 = jax.nn.softmax(jnp.where(mask, s1, -1e30), axis=-1)
        p2 = jax.nn.softmax(jnp.where(mask, s2, -1e30), axis=-1)
        return jnp.einsum('bhqk,bhkd->bqhd', p1 - lam * p2, vf)

    out = lax.map(block, jnp.arange(nb))
    return jnp.transpose(out, (1, 0, 2, 3, 4)).reshape(b, s, h, ATT_V_DIM)


def multiscale_pool(p, w_pool, pool_scale):
    b, s, _ = p.shape
    pf = p.astype(jnp.float32)
    csum = jnp.cumsum(pf, axis=1)
    t1 = jnp.arange(1, s + 1, dtype=jnp.float32)[None, :, None]
    groups = []
    for g, w in enumerate(POOL_WINDOWS):
        cg = csum[..., g * POOL_GDIM:(g + 1) * POOL_GDIM]
        prev = jnp.pad(cg, ((0, 0), (w, 0), (0, 0)))[:, :s]
        groups.append((cg - prev) / jnp.minimum(t1, float(w)))
    pooled = jnp.concatenate(groups, axis=-1) - pf
    mixed = jnp.einsum('bsgc,gcd->bsgd', pooled.reshape(b, s, POOL_GROUPS, POOL_GDIM),
                       w_pool.astype(jnp.float32)).reshape(b, s, POOL_WIDTH)
    return mixed * pool_scale.astype(jnp.float32)


def short_gated_conv(gb, gc, hin, conv_w):
    s = hin.shape[1]
    u = (gc * hin).astype(jnp.float32)
    up = jnp.pad(u, ((0, 0), (CONV_K - 1, 0), (0, 0)))
    wf = conv_w.astype(jnp.float32)
    y = sum(wf[k] * up[:, k:k + s] for k in range(CONV_K))
    return gb.astype(jnp.float32) * y


def setup_inputs(seed: int = 0) -> dict:
    key = jax.random.key(seed)
    ks = jax.random.split(key, 20)
    f32 = jnp.float32
    nrm = lambda k, shape, sc: jax.random.normal(k, shape, f32) * sc
    return {
        "x": nrm(ks[0], (BATCH, SEQ, D_MODEL), 1.0),
        "g_mix": 1.0 + nrm(ks[1], (DEPTH, D_MODEL), 0.01),
        "w_in": nrm(ks[2], (DEPTH, D_MODEL, IN_COLS), D_MODEL ** -0.5),
        "lambda_q1": nrm(ks[3], (DEPTH, ATT_QK_DIM), 0.1),
        "lambda_k1": nrm(ks[4], (DEPTH, ATT_QK_DIM), 0.1),
        "lambda_q2": nrm(ks[5], (DEPTH, ATT_QK_DIM), 0.1),
        "lambda_k2": nrm(ks[6], (DEPTH, ATT_QK_DIM), 0.1),
        "subln_g": 1.0 + nrm(ks[7], (DEPTH, ATT_V_DIM), 0.01),
        "rel_bias": nrm(ks[8], (NUM_BUCKETS, ATT_HEADS), 0.5),
        "w_pool": nrm(ks[9], (DEPTH, POOL_GROUPS, POOL_GDIM, POOL_GDIM), POOL_GDIM ** -0.5),
        "pool_scale": 1.0 + nrm(ks[10], (DEPTH, POOL_WIDTH), 0.1),
        "conv_w": nrm(ks[11], (DEPTH, CONV_K, CONV_WIDTH), CONV_K ** -0.5),
        "w_o": nrm(ks[12], (DEPTH, MIX_WIDTH, D_MODEL), MIX_WIDTH ** -0.5),
        "g_ffn": 1.0 + nrm(ks[13], (DEPTH, D_MODEL), 0.01),
        "w_gate": nrm(ks[14], (DEPTH, D_MODEL, D_FF), D_MODEL ** -0.5),
        "w_up": nrm(ks[15], (DEPTH, D_MODEL, D_FF), D_MODEL ** -0.5),
        "w_down": nrm(ks[16], (DEPTH, D_FF, D_MODEL), D_FF ** -0.5),
        "g_final": 1.0 + nrm(ks[17], (D_MODEL,), 0.01),
    }


def reference(x, g_mix, w_in, lambda_q1, lambda_k1, lambda_q2, lambda_k2, subln_g, rel_bias,
              w_pool, pool_scale, conv_w, w_o, g_ffn, w_gate, w_up, w_down, g_final):
    b, s, _ = x.shape
    split_pts = np.cumsum([QK_COLS, QK_COLS, ATT_WIDTH, POOL_WIDTH, CONV_WIDTH, CONV_WIDTH])
    for l in range(DEPTH):
        h = rmsnorm(x, g_mix[l])
        proj = h @ w_in[l]
        q, k, v, p, gb, gc, hin = jnp.split(proj, list(split_pts), axis=-1)
        q = q.reshape(b, s, ATT_HEADS, 2, ATT_QK_DIM)
        k = k.reshape(b, s, ATT_HEADS, 2, ATT_QK_DIM)
        q1 = jnp.transpose(q[:, :, :, 0], (0, 2, 1, 3))
        q2 = jnp.transpose(q[:, :, :, 1], (0, 2, 1, 3))
        k1 = jnp.transpose(k[:, :, :, 0], (0, 2, 1, 3))
        k2 = jnp.transpose(k[:, :, :, 1], (0, 2, 1, 3))
        vh = jnp.transpose(v.reshape(b, s, ATT_HEADS, ATT_V_DIM), (0, 2, 1, 3))
        lam_init = 0.8 - 0.6 * math.exp(-0.3 * l)
        lam = (jnp.exp(jnp.sum(lambda_q1[l].astype(jnp.float32) * lambda_k1[l].astype(jnp.float32)))
               - jnp.exp(jnp.sum(lambda_q2[l].astype(jnp.float32) * lambda_k2[l].astype(jnp.float32)))
               + lam_init)
        att = diff_attention(q1, q2, k1, k2, vh, rel_bias, lam)
        att = rmsnorm(att, subln_g[l], SUBLN_EPS) * (1.0 - lam_init)
        att = att.reshape(b, s, ATT_WIDTH)
        pool = multiscale_pool(p, w_pool[l], pool_scale[l])
        conv = short_gated_conv(gb, gc, hin, conv_w[l])
        mixed = jnp.concatenate([att, pool, conv], axis=-1).astype(x.dtype)
        x = x + (mixed @ w_o[l]).astype(x.dtype)
        h = rmsnorm(x, g_ffn[l])
        ff = (jax.nn.silu(h @ w_gate[l]) * (h @ w_up[l])) @ w_down[l]
        x = x + ff.astype(x.dtype)
    return rmsnorm(x, g_final)
```

```python
import functools
import math

import numpy as np
import jax
import jax.numpy as jnp
from jax import lax
from jax.experimental import pallas as pl
from jax.experimental.pallas import tpu as pltpu

F32 = jnp.float32
BF16 = jnp.bfloat16

D_MODEL = 1024
SEQ = 4096
ATT_HEADS = 4
QK_DIM = 64
V_DIM = 2 * QK_DIM
ATT_WIDTH = ATT_HEADS * V_DIM
POOL_WINDOWS = (2, 4, 8, 16)
POOL_WIDTH = 256
POOL_GDIM = 64
CONV_WIDTH = 256
CONV_K = 3
IN_COLS = 3 * ATT_WIDTH + POOL_WIDTH + 3 * CONV_WIDTH
D_FF = 2816
NUM_BUCKETS = 32
MAX_EXACT = 16
MAX_DISTANCE = 128
EPS = 1e-6
SUBLN_EPS = 1e-5
NEG = -1e30

HALO = 16
TM_IN = 512
TQ = 512
TM_FFN = 256
VMEM_LIMIT = 56 * 1024 * 1024


def _bucket_thresholds():
    n = np.arange(MAX_DISTANCE)
    nf = np.maximum(n, 1).astype(np.float32)
    large = MAX_EXACT + (np.log(nf / MAX_EXACT) / math.log(MAX_DISTANCE / MAX_EXACT)
                         * (NUM_BUCKETS - MAX_EXACT)).astype(np.int32)
    bucket = np.where(n < MAX_EXACT, n, np.minimum(large, NUM_BUCKETS - 1))
    assert bucket[-1] == NUM_BUCKETS - 1 and np.all(np.diff(bucket) >= 0)
    return [int(np.argmax(bucket >= b)) for b in range(NUM_BUCKETS)]


_BUCKET_THR = _bucket_thresholds()


def _rms(x, g, eps):
    return (x * lax.rsqrt(jnp.mean(x * x, axis=-1, keepdims=True) + eps)) * g


def _bias_kernel(relb_ref, out_ref):
    h = pl.program_id(0)
    which = pl.program_id(1)
    r = lax.broadcasted_iota(jnp.int32, (TQ, TQ), 0)
    c = lax.broadcasted_iota(jnp.int32, (TQ, TQ), 1)
    dist = r - c + which * TQ
    val = jnp.full((TQ, TQ), relb_ref[0, h], F32)
    for b in range(1, NUM_BUCKETS):
        val = jnp.where(dist >= _BUCKET_THR[b], relb_ref[b, h], val)
    out_ref[...] = val - relb_ref[NUM_BUCKETS - 1, h]


def _bias_tiles(rel_bias):
    return pl.pallas_call(
        _bias_kernel,
        grid=(ATT_HEADS, 2),
        in_specs=[pl.BlockSpec(memory_space=pltpu.SMEM)],
        out_specs=pl.BlockSpec((None, None, TQ, TQ), lambda h, w: (h, w, 0, 0)),
        out_shape=jax.ShapeDtypeStruct((ATT_HEADS, 2, TQ, TQ), F32),
        name="bias_tiles",
    )(rel_bias)


def _mix_in_kernel(x_ref, g_ref, w_ref, wbd_ref, pscale_ref, convw_ref,
                   q_ref, k_ref, v_ref, pc_ref, carry_ref):
    tm = x_ref.shape[0]
    tiles_per_seq = SEQ // tm
    seq_tile = pl.program_id(0) % tiles_per_seq

    h = _rms(x_ref[...], g_ref[...], EPS).astype(BF16)
    proj = jnp.dot(h, w_ref[...], preferred_element_type=F32)

    q_ref[...] = (proj[:, 0:ATT_WIDTH] * (QK_DIM ** -0.5)).astype(BF16)
    k_ref[...] = proj[:, ATT_WIDTH:2 * ATT_WIDTH].astype(BF16)
    v_ref[...] = proj[:, 2 * ATT_WIDTH:3 * ATT_WIDTH].astype(BF16)
    o = 3 * ATT_WIDTH
    p = proj[:, o:o + POOL_WIDTH]
    gb = proj[:, o + POOL_WIDTH:o + POOL_WIDTH + CONV_WIDTH]
    gc = proj[:, o + POOL_WIDTH + CONV_WIDTH:o + POOL_WIDTH + 2 * CONV_WIDTH]
    hin = proj[:, o + POOL_WIDTH + 2 * CONV_WIDTH:]
    u = gc * hin

    @pl.when(seq_tile == 0)
    def _():
        carry_ref[...] = jnp.zeros_like(carry_ref)

    p_ext = jnp.concatenate([carry_ref[:, :POOL_WIDTH], p], axis=0)
    u_ext = jnp.concatenate([carry_ref[:, POOL_WIDTH:], u], axis=0)
    carry_ref[:, :POOL_WIDTH] = p[tm - HALO:]
    carry_ref[:, POOL_WIDTH:] = u[tm - HALO:]

    a2 = p_ext + pltpu.roll(p_ext, 1, axis=0)
    a4 = a2 + pltpu.roll(a2, 2, axis=0)
    a4_hi = a4[:, 128:]
    a8 = a4_hi + pltpu.roll(a4_hi, 4, axis=0)
    a16 = a8 + pltpu.roll(a8, 8, axis=0)
    lane = lax.broadcasted_iota(jnp.int32, (tm, 128), 1)
    low = lane < POOL_GDIM
    sums = jnp.concatenate([jnp.where(low, a2[HALO:, :128], a4[HALO:, :128]),
                            jnp.where(low, a8[HALO:], a16[HALO:])], axis=1)
    col = lax.broadcasted_iota(jnp.int32, (tm, POOL_WIDTH), 1)
    win = jnp.left_shift(2, col // POOL_GDIM)
    t1 = seq_tile * tm + lax.broadcasted_iota(jnp.int32, (tm, POOL_WIDTH), 0) + 1
    pooled = sums / jnp.minimum(t1, win).astype(F32) - p
    mixed = jnp.dot(pooled.astype(BF16), wbd_ref[...], preferred_element_type=F32) * pscale_ref[...]
    pc_ref[:, :POOL_WIDTH] = mixed.astype(BF16)

    y = (convw_ref[0:1, :] * pltpu.roll(u_ext, 2, axis=0)[HALO:]
         + convw_ref[1:2, :] * pltpu.roll(u_ext, 1, axis=0)[HALO:]
         + convw_ref[2:3, :] * u)
    pc_ref[:, POOL_WIDTH:] = (gb * y).astype(BF16)


def _mix_in(x2d, g, w_in, wbd, pscale, convw):
    t = x2d.shape[0]
    tm = TM_IN
    const = lambda shape: pl.BlockSpec(shape, lambda i: (0, 0), pipeline_mode=pl.Buffered(1))
    row = lambda width: pl.BlockSpec((tm, width), lambda i: (i, 0))
    return pl.pallas_call(
        _mix_in_kernel,
        grid=(t // tm,),
        in_specs=[row(D_MODEL), const((1, D_MODEL)), const((D_MODEL, IN_COLS)),
                  const((POOL_WIDTH, POOL_WIDTH)), const((1, POOL_WIDTH)), const((CONV_K, CONV_WIDTH))],
        out_specs=[row(ATT_WIDTH)] * 4,
        out_shape=[jax.ShapeDtypeStruct((t, ATT_WIDTH), BF16)] * 4,
        scratch_shapes=[pltpu.VMEM((HALO, POOL_WIDTH + CONV_WIDTH), F32)],
        compiler_params=pltpu.CompilerParams(dimension_semantics=("arbitrary",),
                                             vmem_limit_bytes=VMEM_LIMIT),
        name="mix_in",
    )(x2d, g, w_in, wbd, pscale, convw)


def _attn_kernel(lq1_ref, lk1_ref, lq2_ref, lk2_ref, g_ref, q_ref, k_ref, v_ref, bias_ref, o_ref,
                 qs_ref, m_ref, l_ref, acc_ref, *, lam_init):
    i = pl.program_id(2)
    q = q_ref[...]
    lane = lax.broadcasted_iota(jnp.int32, q.shape, 1)
    zero = jnp.zeros_like(q)
    qs_ref[:TQ] = jnp.where(lane < QK_DIM, q, zero)
    qs_ref[TQ:] = jnp.where(lane >= QK_DIM, q, zero)
    m_ref[...] = jnp.full_like(m_ref, NEG)
    l_ref[...] = jnp.zeros_like(l_ref)
    acc_ref[...] = jnp.zeros_like(acc_ref)

    def step(j, bias, masked):
        kb = k_ref[pl.ds(pl.multiple_of(j * TQ, TQ), TQ), :]
        vb = v_ref[pl.ds(pl.multiple_of(j * TQ, TQ), TQ), :]
        s = lax.dot_general(qs_ref[...], kb, (((1,), (1,)), ((), ())),
                            preferred_element_type=F32)
        if bias is not None:
            s = (s.reshape(2, TQ, TQ) + bias[None]).reshape(2 * TQ, TQ)
        if masked:
            r = lax.broadcasted_iota(jnp.int32, (2, TQ, TQ), 1)
            c = lax.broadcasted_iota(jnp.int32, (2, TQ, TQ), 2)
            s = jnp.where(r >= c, s.reshape(2, TQ, TQ), NEG).reshape(2 * TQ, TQ)
        m_old = m_ref[...]
        m_new = jnp.maximum(m_old, jnp.max(s, axis=-1, keepdims=True))
        p = jnp.exp(s - m_new)
        alpha = jnp.exp(m_old - m_new)
        l_ref[...] = alpha * l_ref[...] + jnp.sum(p, axis=-1, keepdims=True)
        acc_ref[...] = alpha * acc_ref[...] + jnp.dot(p.astype(BF16), vb, preferred_element_type=F32)
        m_ref[...] = m_new

    def plain(j, carry):
        step(j, None, False)
        return carry

    lax.fori_loop(0, jnp.maximum(i - 1, 0), plain, 0)

    @pl.when(i >= 1)
    def _():
        step(i - 1, bias_ref[1], False)

    step(i, bias_ref[0], True)

    lam = (jnp.exp(jnp.sum(lq1_ref[...] * lk1_ref[...], axis=-1, keepdims=True))
           - jnp.exp(jnp.sum(lq2_ref[...] * lk2_ref[...], axis=-1, keepdims=True)) + lam_init)
    out = acc_ref[...] / l_ref[...]
    att = out[:TQ] - lam * out[TQ:]
    o_ref[...] = (_rms(att, g_ref[...], SUBLN_EPS) * (1.0 - lam_init)).astype(BF16)


def _attn(q, k, v, bias, lq1, lk1, lq2, lk2, subln_g, lam_init, batch):
    t = q.shape[0]
    nq = SEQ // TQ
    vec = lambda n: pl.BlockSpec((1, n), lambda b, h, i: (0, 0))
    seq = pl.BlockSpec((SEQ, V_DIM), lambda b, h, i: (b, h))
    tile = pl.BlockSpec((TQ, V_DIM), lambda b, h, i: (b * nq + i, h))
    return pl.pallas_call(
        functools.partial(_attn_kernel, lam_init=lam_init),
        grid=(batch, ATT_HEADS, nq),
        in_specs=[vec(QK_DIM)] * 4 + [vec(V_DIM), tile, seq, seq,
                  pl.BlockSpec((None, 2, TQ, TQ), lambda b, h, i: (h, 0, 0, 0))],
        out_specs=tile,
        out_shape=jax.ShapeDtypeStruct((t, ATT_WIDTH), BF16),
        scratch_shapes=[pltpu.VMEM((2 * TQ, V_DIM), BF16), pltpu.VMEM((2 * TQ, 1), F32),
                        pltpu.VMEM((2 * TQ, 1), F32), pltpu.VMEM((2 * TQ, V_DIM), F32)],
        compiler_params=pltpu.CompilerParams(dimension_semantics=("arbitrary",) * 3,
                                             vmem_limit_bytes=VMEM_LIMIT),
        name="attn",
    )(lq1, lk1, lq2, lk2, subln_g, q, k, v, bias)


def _out_ffn_kernel(x_ref, att_ref, pc_ref, woa_ref, wob_ref, gffn_ref, wg_ref, wu_ref, wd_ref,
                    gfin_ref, o_ref, *, final):
    y = (jnp.dot(att_ref[...], woa_ref[...], preferred_element_type=F32)
         + jnp.dot(pc_ref[...], wob_ref[...], preferred_element_type=F32))
    x1 = x_ref[...] + y
    h = _rms(x1, gffn_ref[...], EPS).astype(BF16)
    gate = jnp.dot(h, wg_ref[...], preferred_element_type=F32)
    up = jnp.dot(h, wu_ref[...], preferred_element_type=F32)
    a = (jax.nn.silu(gate) * up).astype(BF16)
    x2 = x1 + jnp.dot(a, wd_ref[...], preferred_element_type=F32)
    o_ref[...] = _rms(x2, gfin_ref[...], EPS) if final else x2


def _out_ffn(x2d, att, pc, woa, wob, gffn, wg, wu, wd, gfin, final):
    t = x2d.shape[0]
    tm = TM_FFN
    const = lambda shape: pl.BlockSpec(shape, lambda i: (0, 0), pipeline_mode=pl.Buffered(1))
    row = lambda width: pl.BlockSpec((tm, width), lambda i: (i, 0))
    return pl.pallas_call(
        functools.partial(_out_ffn_kernel, final=final),
        grid=(t // tm,),
        in_specs=[row(D_MODEL), row(ATT_WIDTH), row(ATT_WIDTH),
                  const((ATT_WIDTH, D_MODEL)), const((ATT_WIDTH, D_MODEL)), const((1, D_MODEL)),
                  const((D_MODEL, D_FF)), const((D_MODEL, D_FF)), const((D_FF, D_MODEL)),
                  const((1, D_MODEL))],
        out_specs=row(D_MODEL),
        out_shape=jax.ShapeDtypeStruct((t, D_MODEL), F32),
        compiler_params=pltpu.CompilerParams(dimension_semantics=("arbitrary",),
                                             vmem_limit_bytes=VMEM_LIMIT),
        name="out_ffn",
    )(x2d, att, pc, woa, wob, gffn, wg, wu, wd, gfin)


def _block_diag(w_pool):
    groups = w_pool.shape[0]
    out = jnp.zeros((POOL_WIDTH, POOL_WIDTH), w_pool.dtype)
    for g in range(groups):
        out = out.at[g * POOL_GDIM:(g + 1) * POOL_GDIM, g * POOL_GDIM:(g + 1) * POOL_GDIM].set(w_pool[g])
    return out


def kernel(x, g_mix, w_in, lambda_q1, lambda_k1, lambda_q2, lambda_k2, subln_g, rel_bias,
           w_pool, pool_scale, conv_w, w_o, g_ffn, w_gate, w_up, w_down, g_final):
    batch, seq, d = x.shape
    assert (seq, d) == (SEQ, D_MODEL)
    depth = w_in.shape[0]
    xf = x.reshape(batch * seq, d)
    bias = _bias_tiles(rel_bias.astype(F32))
    for l in range(depth):
        lam_init = 0.8 - 0.6 * math.exp(-0.3 * l)
        q, k, v, pc = _mix_in(xf, g_mix[l][None], w_in[l].astype(BF16), _block_diag(w_pool[l]).astype(BF16),
                              pool_scale[l][None], conv_w[l])
        att = _attn(q, k, v, bias, lambda_q1[l][None], lambda_k1[l][None], lambda_q2[l][None],
                    lambda_k2[l][None], subln_g[l][None], lam_init, batch)
        wo = w_o[l].astype(BF16)
        xf = _out_ffn(xf, att, pc, wo[:ATT_WIDTH], wo[ATT_WIDTH:], g_ffn[l][None],
                      w_gate[l].astype(BF16), w_up[l].astype(BF16), w_down[l].astype(BF16),
                      g_final[None], final=(l == depth - 1))
    return xf.reshape(batch, seq, d)
```

```python
import functools
import math

import numpy as np
import jax
import jax.numpy as jnp
from jax import lax
from jax.experimental import pallas as pl
from jax.experimental.pallas import tpu as pltpu

F32 = jnp.float32
BF16 = jnp.bfloat16

D_MODEL = 1024
SEQ = 4096
ATT_HEADS = 4
QK_DIM = 64
V_DIM = 2 * QK_DIM
ATT_WIDTH = ATT_HEADS * V_DIM
POOL_WINDOWS = (2, 4, 8, 16)
POOL_WIDTH = 256
POOL_GDIM = 64
CONV_WIDTH = 256
CONV_K = 3
IN_COLS = 3 * ATT_WIDTH + POOL_WIDTH + 3 * CONV_WIDTH
D_FF = 2816
NUM_BUCKETS = 32
MAX_EXACT = 16
MAX_DISTANCE = 128
EPS = 1e-6
SUBLN_EPS = 1e-5
NEG = -1e30
LOG2E = math.log2(math.e)

HALO = 16
TQ = 512
TM_IN = TQ
TM_FFN = 256
VMEM_LIMIT = 56 * 1024 * 1024


def _bucket_thresholds():
    n = np.arange(MAX_DISTANCE)
    nf = np.maximum(n, 1).astype(np.float32)
    large = MAX_EXACT + (np.log(nf / MAX_EXACT) / math.log(MAX_DISTANCE / MAX_EXACT)
                         * (NUM_BUCKETS - MAX_EXACT)).astype(np.int32)
    bucket = np.where(n < MAX_EXACT, n, np.minimum(large, NUM_BUCKETS - 1))
    assert bucket[-1] == NUM_BUCKETS - 1 and np.all(np.diff(bucket) >= 0)
    return [int(np.argmax(bucket >= b)) for b in range(NUM_BUCKETS)]


_BUCKET_THR = _bucket_thresholds()


def _rms(x, g, eps):
    return (x * lax.rsqrt(jnp.mean(x * x, axis=-1, keepdims=True) + eps)) * g


def _bias_kernel(relb_ref, out_ref):
    h = pl.program_id(0)
    which = pl.program_id(1)
    kpos = lax.broadcasted_iota(jnp.int32, (TQ, TQ), 0)
    qpos = lax.broadcasted_iota(jnp.int32, (TQ, TQ), 1)
    dist = qpos - kpos + which * TQ
    val = jnp.full((TQ, TQ), relb_ref[0, h], F32)
    for b in range(1, NUM_BUCKETS):
        val = jnp.where(dist >= _BUCKET_THR[b], relb_ref[b, h], val)
    val = jnp.where(dist >= 0, (val - relb_ref[NUM_BUCKETS - 1, h]) * LOG2E, NEG)
    out_ref[:, :TQ] = val
    out_ref[:, TQ:] = val


def _bias_tiles(rel_bias):
    return pl.pallas_call(
        _bias_kernel,
        grid=(ATT_HEADS, 2),
        in_specs=[pl.BlockSpec(memory_space=pltpu.SMEM)],
        out_specs=pl.BlockSpec((None, None, TQ, 2 * TQ), lambda h, w: (h, w, 0, 0)),
        out_shape=jax.ShapeDtypeStruct((ATT_HEADS, 2, TQ, 2 * TQ), F32),
        name="bias_tiles",
    )(rel_bias)


def _mix_in_kernel(x_ref, g_ref, w_ref, wbd_ref, pscale_ref, convw_ref,
                   q_ref, k_ref, vt_ref, pc_ref, carry_ref):
    tm = x_ref.shape[0]
    tiles_per_seq = SEQ // tm
    seq_tile = pl.program_id(0) % tiles_per_seq

    h = _rms(x_ref[...], g_ref[...], EPS).astype(BF16)
    proj = jnp.dot(h, w_ref[...], preferred_element_type=F32)

    q_ref[...] = (proj[:, 0:ATT_WIDTH] * (QK_DIM ** -0.5 * LOG2E)).astype(BF16)
    k_ref[...] = proj[:, ATT_WIDTH:2 * ATT_WIDTH].astype(BF16)
    vt_ref[...] = proj[:, 2 * ATT_WIDTH:3 * ATT_WIDTH].T.astype(BF16)
    o = 3 * ATT_WIDTH
    p = proj[:, o:o + POOL_WIDTH]
    gb = proj[:, o + POOL_WIDTH:o + POOL_WIDTH + CONV_WIDTH]
    gc = proj[:, o + POOL_WIDTH + CONV_WIDTH:o + POOL_WIDTH + 2 * CONV_WIDTH]
    hin = proj[:, o + POOL_WIDTH + 2 * CONV_WIDTH:]
    u = gc * hin

    @pl.when(seq_tile == 0)
    def _():
        carry_ref[...] = jnp.zeros_like(carry_ref)

    p_ext = jnp.concatenate([carry_ref[:, :POOL_WIDTH], p], axis=0)
    u_ext = jnp.concatenate([carry_ref[:, POOL_WIDTH:], u], axis=0)
    carry_ref[:, :POOL_WIDTH] = p[tm - HALO:]
    carry_ref[:, POOL_WIDTH:] = u[tm - HALO:]

    a2 = p_ext + pltpu.roll(p_ext, 1, axis=0)
    a4 = a2 + pltpu.roll(a2, 2, axis=0)
    a4_hi = a4[:, 128:]
    a8 = a4_hi + pltpu.roll(a4_hi, 4, axis=0)
    a16 = a8 + pltpu.roll(a8, 8, axis=0)
    lane = lax.broadcasted_iota(jnp.int32, (tm, 128), 1)
    low = lane < POOL_GDIM
    sums = jnp.concatenate([jnp.where(low, a2[HALO:, :128], a4[HALO:, :128]),
                            jnp.where(low, a8[HALO:], a16[HALO:])], axis=1)
    col = lax.broadcasted_iota(jnp.int32, (tm, POOL_WIDTH), 1)
    win = jnp.left_shift(2, col // POOL_GDIM)
    t1 = seq_tile * tm + lax.broadcasted_iota(jnp.int32, (tm, POOL_WIDTH), 0) + 1
    pooled = sums / jnp.minimum(t1, win).astype(F32) - p
    mixed = jnp.dot(pooled.astype(BF16), wbd_ref[...], preferred_element_type=F32) * pscale_ref[...]
    pc_ref[:, :POOL_WIDTH] = mixed.astype(BF16)

    y = (convw_ref[0:1, :] * pltpu.roll(u_ext, 2, axis=0)[HALO:]
         + convw_ref[1:2, :] * pltpu.roll(u_ext, 1, axis=0)[HALO:]
         + convw_ref[2:3, :] * u)
    pc_ref[:, POOL_WIDTH:] = (gb * y).astype(BF16)


def _mix_in(x2d, g, w_in, wbd, pscale, convw):
    t = x2d.shape[0]
    tm = TM_IN
    const = lambda shape: pl.BlockSpec(shape, lambda i: (0, 0), pipeline_mode=pl.Buffered(1))
    row = lambda width: pl.BlockSpec((tm, width), lambda i: (i, 0))
    return pl.pallas_call(
        _mix_in_kernel,
        grid=(t // tm,),
        in_specs=[row(D_MODEL), const((1, D_MODEL)), const((D_MODEL, IN_COLS)),
                  const((POOL_WIDTH, POOL_WIDTH)), const((1, POOL_WIDTH)), const((CONV_K, CONV_WIDTH))],
        out_specs=[row(ATT_WIDTH), row(ATT_WIDTH),
                   pl.BlockSpec((None, ATT_WIDTH, tm), lambda i: (i, 0, 0)), row(ATT_WIDTH)],
        out_shape=[jax.ShapeDtypeStruct((t, ATT_WIDTH), BF16), jax.ShapeDtypeStruct((t, ATT_WIDTH), BF16),
                   jax.ShapeDtypeStruct((t // tm, ATT_WIDTH, tm), BF16),
                   jax.ShapeDtypeStruct((t, ATT_WIDTH), BF16)],
        scratch_shapes=[pltpu.VMEM((HALO, POOL_WIDTH + CONV_WIDTH), F32)],
        compiler_params=pltpu.CompilerParams(dimension_semantics=("arbitrary",),
                                             vmem_limit_bytes=VMEM_LIMIT),
        name="mix_in",
    )(x2d, g, w_in, wbd, pscale, convw)


def _attn_kernel(lq1_ref, lk1_ref, lq2_ref, lk2_ref, g_ref, q_ref, k_ref, vt_ref, bias_ref, o_ref,
                 qs_ref, m_ref, l_ref, acc_ref, *, lam_init):
    i = pl.program_id(2)
    q = q_ref[...]
    lane = lax.broadcasted_iota(jnp.int32, q.shape, 1)
    zero = jnp.zeros_like(q)
    qs_ref[:TQ] = jnp.where(lane < QK_DIM, q, zero)
    qs_ref[TQ:] = jnp.where(lane >= QK_DIM, q, zero)
    m_ref[...] = jnp.full_like(m_ref, NEG)
    l_ref[...] = jnp.zeros_like(l_ref)
    acc_ref[...] = jnp.zeros_like(acc_ref)

    def step(j, bias):
        kb = k_ref[pl.ds(pl.multiple_of(j * TQ, TQ), TQ), :]
        s = lax.dot_general(kb, qs_ref[...], (((1,), (1,)), ((), ())),
                            preferred_element_type=F32)
        if bias is not None:
            s = s + bias
        m_old = m_ref[...]
        m_new = jnp.maximum(m_old, jnp.max(s, axis=0, keepdims=True))
        p = jnp.exp2(s - m_new)
        alpha = jnp.exp2(m_old - m_new)
        l_ref[...] = alpha * l_ref[...] + jnp.sum(p, axis=0, keepdims=True)
        acc_ref[...] = alpha * acc_ref[...] + jnp.dot(vt_ref[j], p.astype(BF16),
                                                      preferred_element_type=F32)
        m_ref[...] = m_new

    def plain(j, carry):
        step(j, None)
        return carry

    lax.fori_loop(0, jnp.maximum(i - 1, 0), plain, 0)

    @pl.when(i >= 1)
    def _():
        step(i - 1, bias_ref[1])

    step(i, bias_ref[0])

    lam = (jnp.exp(jnp.sum(lq1_ref[...] * lk1_ref[...], axis=-1, keepdims=True))
           - jnp.exp(jnp.sum(lq2_ref[...] * lk2_ref[...], axis=-1, keepdims=True)) + lam_init)
    out = acc_ref[...] / l_ref[...]
    att = out[:, :TQ] - lam * out[:, TQ:]
    y = (att * lax.rsqrt(jnp.mean(att * att, axis=0, keepdims=True) + SUBLN_EPS)) * g_ref[...]
    o_ref[...] = (y * (1.0 - lam_init)).T.astype(BF16)


def _attn(q, k, vt, bias, lq1, lk1, lq2, lk2, subln_g, lam_init, batch):
    t = q.shape[0]
    nq = SEQ // TQ
    vec = lambda n: pl.BlockSpec((1, n), lambda h, b, i: (0, 0))
    tile = pl.BlockSpec((TQ, V_DIM), lambda h, b, i: (b * nq + i, h))
    return pl.pallas_call(
        functools.partial(_attn_kernel, lam_init=lam_init),
        grid=(ATT_HEADS, batch, nq),
        in_specs=[vec(QK_DIM)] * 4 + [
            pl.BlockSpec((V_DIM, 1), lambda h, b, i: (0, 0)),
            tile,
            pl.BlockSpec((SEQ, V_DIM), lambda h, b, i: (b, h)),
            pl.BlockSpec((nq, V_DIM, TQ), lambda h, b, i: (b, h, 0)),
            pl.BlockSpec((None, 2, TQ, 2 * TQ), lambda h, b, i: (h, 0, 0, 0))],
        out_specs=tile,
        out_shape=jax.ShapeDtypeStruct((t, ATT_WIDTH), BF16),
        scratch_shapes=[pltpu.VMEM((2 * TQ, V_DIM), BF16), pltpu.VMEM((1, 2 * TQ), F32),
                        pltpu.VMEM((1, 2 * TQ), F32), pltpu.VMEM((V_DIM, 2 * TQ), F32)],
        compiler_params=pltpu.CompilerParams(dimension_semantics=("arbitrary",) * 3,
                                             vmem_limit_bytes=VMEM_LIMIT),
        name="attn",
    )(lq1, lk1, lq2, lk2, subln_g, q, k, vt, bias)


def _out_ffn_kernel(x_ref, att_ref, pc_ref, woa_ref, wob_ref, gffn_ref, wg_ref, wu_ref, wd_ref,
                    gfin_ref, o_ref, *, final):
    y = (jnp.dot(att_ref[...], woa_ref[...], preferred_element_type=F32)
         + jnp.dot(pc_ref[...], wob_ref[...], preferred_element_type=F32))
    x1 = x_ref[...] + y
    h = _rms(x1, gffn_ref[...], EPS).astype(BF16)
    gate = jnp.dot(h, wg_ref[...], preferred_element_type=F32)
    up = jnp.dot(h, wu_ref[...], preferred_element_type=F32)
    a = (jax.nn.silu(gate) * up).astype(BF16)
    x2 = x1 + jnp.dot(a, wd_ref[...], preferred_element_type=F32)
    o_ref[...] = _rms(x2, gfin_ref[...], EPS) if final else x2


def _out_ffn(x2d, att, pc, woa, wob, gffn, wg, wu, wd, gfin, final):
    t = x2d.shape[0]
    tm = TM_FFN
    const = lambda shape: pl.BlockSpec(shape, lambda i: (0, 0), pipeline_mode=pl.Buffered(1))
    row = lambda width: pl.BlockSpec((tm, width), lambda i: (i, 0))
    return pl.pallas_call(
        functools.partial(_out_ffn_kernel, final=final),
        grid=(t // tm,),
        in_specs=[row(D_MODEL), row(ATT_WIDTH), row(ATT_WIDTH),
                  const((ATT_WIDTH, D_MODEL)), const((ATT_WIDTH, D_MODEL)), const((1, D_MODEL)),
                  const((D_MODEL, D_FF)), const((D_MODEL, D_FF)), const((D_FF, D_MODEL)),
                  const((1, D_MODEL))],
        out_specs=row(D_MODEL),
        out_shape=jax.ShapeDtypeStruct((t, D_MODEL), F32),
        compiler_params=pltpu.CompilerParams(dimension_semantics=("arbitrary",),
                                             vmem_limit_bytes=VMEM_LIMIT),
        name="out_ffn",
    )(x2d, att, pc, woa, wob, gffn, wg, wu, wd, gfin)


def _block_diag(w_pool):
    groups = w_pool.shape[0]
    out = jnp.zeros((POOL_WIDTH, POOL_WIDTH), w_pool.dtype)
    for g in range(groups):
        out = out.at[g * POOL_GDIM:(g + 1) * POOL_GDIM, g * POOL_GDIM:(g + 1) * POOL_GDIM].set(w_pool[g])
    return out


def kernel(x, g_mix, w_in, lambda_q1, lambda_k1, lambda_q2, lambda_k2, subln_g, rel_bias,
           w_pool, pool_scale, conv_w, w_o, g_ffn, w_gate, w_up, w_down, g_final):
    batch, seq, d = x.shape
    assert (seq, d) == (SEQ, D_MODEL)
    depth = w_in.shape[0]
    xf = x.reshape(batch * seq, d)
    bias = _bias_tiles(rel_bias.astype(F32))
    for l in range(depth):
        lam_init = 0.8 - 0.6 * math.exp(-0.3 * l)
        q, k, vt, pc = _mix_in(xf, g_mix[l][None], w_in[l].astype(BF16), _block_diag(w_pool[l]).astype(BF16),
                               pool_scale[l][None], conv_w[l])
        att = _attn(q, k, vt, bias, lambda_q1[l][None], lambda_k1[l][None], lambda_q2[l][None],
                    lambda_k2[l][None], subln_g[l][:, None], lam_init, batch)
        wo = w_o[l].astype(BF16)
        xf = _out_ffn(xf, att, pc, wo[:ATT_WIDTH], wo[ATT_WIDTH:], g_ffn[l][None],
                      w_gate[l].astype(BF16), w_up[l].astype(BF16), w_down[l].astype(BF16),
                      g_final[None], final=(l == depth - 1))
    return xf.reshape(batch, seq, d)
```

```python
import functools
import math

import numpy as np
import jax
import jax.numpy as jnp
from jax import lax
from jax.experimental import pallas as pl
from jax.experimental.pallas import tpu as pltpu

F32 = jnp.float32
BF16 = jnp.bfloat16

D_MODEL = 1024
SEQ = 4096
ATT_HEADS = 4
QK_DIM = 64
V_DIM = 2 * QK_DIM
ATT_WIDTH = ATT_HEADS * V_DIM
POOL_WINDOWS = (2, 4, 8, 16)
POOL_WIDTH = 256
POOL_GDIM = 64
CONV_WIDTH = 256
CONV_K = 3
IN_COLS = 3 * ATT_WIDTH + POOL_WIDTH + 3 * CONV_WIDTH
D_FF = 2816
NUM_BUCKETS = 32
MAX_EXACT = 16
MAX_DISTANCE = 128
EPS = 1e-6
SUBLN_EPS = 1e-5
NEG = -1e30
LOG2E = math.log2(math.e)

HALO = 16
TQ = 512
TM_IN = TQ
TM_FFN = 256
VMEM_LIMIT = 56 * 1024 * 1024


def _bucket_thresholds():
    n = np.arange(MAX_DISTANCE)
    nf = np.maximum(n, 1).astype(np.float32)
    large = MAX_EXACT + (np.log(nf / MAX_EXACT) / math.log(MAX_DISTANCE / MAX_EXACT)
                         * (NUM_BUCKETS - MAX_EXACT)).astype(np.int32)
    bucket = np.where(n < MAX_EXACT, n, np.minimum(large, NUM_BUCKETS - 1))
    assert bucket[-1] == NUM_BUCKETS - 1 and np.all(np.diff(bucket) >= 0)
    return [int(np.argmax(bucket >= b)) for b in range(NUM_BUCKETS)]


_BUCKET_THR = _bucket_thresholds()


def _rms(x, g, eps):
    return (x * lax.rsqrt(jnp.mean(x * x, axis=-1, keepdims=True) + eps)) * g


def _bias_kernel(relb_ref, out_ref):
    h = pl.program_id(0)
    which = pl.program_id(1)
    kpos = lax.broadcasted_iota(jnp.int32, (TQ, TQ), 0)
    qpos = lax.broadcasted_iota(jnp.int32, (TQ, TQ), 1)
    dist = qpos - kpos + which * TQ
    val = jnp.full((TQ, TQ), relb_ref[0, h], F32)
    for b in range(1, NUM_BUCKETS):
        val = jnp.where(dist >= _BUCKET_THR[b], relb_ref[b, h], val)
    val = jnp.where(dist >= 0, (val - relb_ref[NUM_BUCKETS - 1, h]) * LOG2E, NEG)
    out_ref[:, :TQ] = val
    out_ref[:, TQ:] = val


def _bias_tiles(rel_bias):
    return pl.pallas_call(
        _bias_kernel,
        grid=(ATT_HEADS, 2),
        in_specs=[pl.BlockSpec(memory_space=pltpu.SMEM)],
        out_specs=pl.BlockSpec((None, None, TQ, 2 * TQ), lambda h, w: (h, w, 0, 0)),
        out_shape=jax.ShapeDtypeStruct((ATT_HEADS, 2, TQ, 2 * TQ), F32),
        name="bias_tiles",
    )(rel_bias)


def _mix_in_kernel(x_ref, g_ref, w_ref, wbd_ref, pscale_ref, convw_ref,
                   q_ref, k_ref, vt_ref, pc_ref, carry_ref):
    tm = x_ref.shape[0]
    tiles_per_seq = SEQ // tm
    seq_tile = pl.program_id(0) % tiles_per_seq

    h = _rms(x_ref[...], g_ref[...], EPS).astype(BF16)
    proj = jnp.dot(h, w_ref[...], preferred_element_type=F32)

    q_ref[...] = (proj[:, 0:ATT_WIDTH] * (QK_DIM ** -0.5 * LOG2E)).astype(BF16)
    k_ref[...] = proj[:, ATT_WIDTH:2 * ATT_WIDTH].astype(BF16)
    vt_ref[...] = proj[:, 2 * ATT_WIDTH:3 * ATT_WIDTH].T.astype(BF16)
    o = 3 * ATT_WIDTH
    p = proj[:, o:o + POOL_WIDTH]
    gb = proj[:, o + POOL_WIDTH:o + POOL_WIDTH + CONV_WIDTH]
    gc = proj[:, o + POOL_WIDTH + CONV_WIDTH:o + POOL_WIDTH + 2 * CONV_WIDTH]
    hin = proj[:, o + POOL_WIDTH + 2 * CONV_WIDTH:]
    u = gc * hin

    @pl.when(seq_tile == 0)
    def _():
        carry_ref[...] = jnp.zeros_like(carry_ref)

    p_ext = jnp.concatenate([carry_ref[:, :POOL_WIDTH], p], axis=0)
    u_ext = jnp.concatenate([carry_ref[:, POOL_WIDTH:], u], axis=0)
    carry_ref[:, :POOL_WIDTH] = p[tm - HALO:]
    carry_ref[:, POOL_WIDTH:] = u[tm - HALO:]

    a2 = p_ext + pltpu.roll(p_ext, 1, axis=0)
    a4 = a2 + pltpu.roll(a2, 2, axis=0)
    a4_hi = a4[:, 128:]
    a8 = a4_hi + pltpu.roll(a4_hi, 4, axis=0)
    a16 = a8 + pltpu.roll(a8, 8, axis=0)
    lane = lax.broadcasted_iota(jnp.int32, (tm, 128), 1)
    low = lane < POOL_GDIM
    sums = jnp.concatenate([jnp.where(low, a2[HALO:, :128], a4[HALO:, :128]),
                            jnp.where(low, a8[HALO:], a16[HALO:])], axis=1)
    col = lax.broadcasted_iota(jnp.int32, (tm, POOL_WIDTH), 1)
    win = jnp.left_shift(2, col // POOL_GDIM)
    t1 = seq_tile * tm + lax.broadcasted_iota(jnp.int32, (tm, POOL_WIDTH), 0) + 1
    pooled = sums / jnp.minimum(t1, win).astype(F32) - p
    mixed = jnp.dot(pooled.astype(BF16), wbd_ref[...], preferred_element_type=F32) * pscale_ref[...]
    pc_ref[:, :POOL_WIDTH] = mixed.astype(BF16)

    y = (convw_ref[0:1, :] * pltpu.roll(u_ext, 2, axis=0)[HALO:]
         + convw_ref[1:2, :] * pltpu.roll(u_ext, 1, axis=0)[HALO:]
         + convw_ref[2:3, :] * u)
    pc_ref[:, POOL_WIDTH:] = (gb * y).astype(BF16)


def _mix_in(x2d, g, w_in, wbd, pscale, convw):
    t = x2d.shape[0]
    tm = TM_IN
    const = lambda shape: pl.BlockSpec(shape, lambda i: (0, 0), pipeline_mode=pl.Buffered(1))
    row = lambda width: pl.BlockSpec((tm, width), lambda i: (i, 0))
    return pl.pallas_call(
        _mix_in_kernel,
        grid=(t // tm,),
        in_specs=[row(D_MODEL), const((1, D_MODEL)), const((D_MODEL, IN_COLS)),
                  const((POOL_WIDTH, POOL_WIDTH)), const((1, POOL_WIDTH)), const((CONV_K, CONV_WIDTH))],
        out_specs=[row(ATT_WIDTH), row(ATT_WIDTH),
                   pl.BlockSpec((None, ATT_WIDTH, tm), lambda i: (i, 0, 0)), row(ATT_WIDTH)],
        out_shape=[jax.ShapeDtypeStruct((t, ATT_WIDTH), BF16), jax.ShapeDtypeStruct((t, ATT_WIDTH), BF16),
                   jax.ShapeDtypeStruct((t // tm, ATT_WIDTH, tm), BF16),
                   jax.ShapeDtypeStruct((t, ATT_WIDTH), BF16)],
        scratch_shapes=[pltpu.VMEM((HALO, POOL_WIDTH + CONV_WIDTH), F32)],
        compiler_params=pltpu.CompilerParams(dimension_semantics=("arbitrary",),
                                             vmem_limit_bytes=VMEM_LIMIT),
        name="mix_in",
    )(x2d, g, w_in, wbd, pscale, convw)


def _attn_kernel(lq1_ref, lk1_ref, lq2_ref, lk2_ref, g_ref, q_ref, k_ref, vt_ref, bias_ref, o_ref,
                 qs_ref, sa_ref, sb_ref, mxa_ref, mxb_ref, m_ref, l_ref, acc_ref, *, lam_init):
    i = pl.program_id(2)
    q = q_ref[...]
    lane = lax.broadcasted_iota(jnp.int32, q.shape, 1)
    zero = jnp.zeros_like(q)
    qs_ref[:TQ] = jnp.where(lane < QK_DIM, q, zero)
    qs_ref[TQ:] = jnp.where(lane >= QK_DIM, q, zero)
    m_ref[...] = jnp.full_like(m_ref, NEG)
    l_ref[...] = jnp.zeros_like(l_ref)
    acc_ref[...] = jnp.zeros_like(acc_ref)
    buf_a = (sa_ref, mxa_ref)
    buf_b = (sb_ref, mxb_ref)

    def scores(j, buf, bias=None):
        s_ref, mx_ref = buf
        kb = k_ref[pl.ds(pl.multiple_of(j * TQ, TQ), TQ), :]
        s = lax.dot_general(kb, qs_ref[...], (((1,), (1,)), ((), ())),
                            preferred_element_type=F32)
        if bias is not None:
            s = s + bias
        s_ref[...] = s
        mx_ref[...] = jnp.max(s, axis=0, keepdims=True)

    def absorb(j, buf):
        s_ref, mx_ref = buf
        m_old = m_ref[...]
        m_new = jnp.maximum(m_old, mx_ref[...])
        p = jnp.exp2(s_ref[...] - m_new)
        alpha = jnp.exp2(m_old - m_new)
        l_ref[...] = alpha * l_ref[...] + jnp.sum(p, axis=0, keepdims=True)
        acc_ref[...] = alpha * acc_ref[...] + jnp.dot(vt_ref[j], p.astype(BF16),
                                                      preferred_element_type=F32)
        m_ref[...] = m_new

    scores(i, buf_a, bias_ref[0])

    @pl.when(i == 0)
    def _():
        absorb(0, buf_a)

    @pl.when(i >= 1)
    def _():
        scores(i - 1, buf_b, bias_ref[1])
        absorb(i, buf_a)
        n_plain = i - 1

        def pair(t, carry):
            jb = i - 1 - 2 * t
            scores(jb - 1, buf_a)
            absorb(jb, buf_b)
            scores(jb - 2, buf_b)
            absorb(jb - 1, buf_a)
            return carry

        lax.fori_loop(0, n_plain // 2, pair, 0)

        @pl.when(n_plain % 2 == 1)
        def _():
            scores(0, buf_a)
            absorb(1, buf_b)
            absorb(0, buf_a)

        @pl.when(n_plain % 2 == 0)
        def _():
            absorb(0, buf_b)

    lam = (jnp.exp(jnp.sum(lq1_ref[...] * lk1_ref[...], axis=-1, keepdims=True))
           - jnp.exp(jnp.sum(lq2_ref[...] * lk2_ref[...], axis=-1, keepdims=True)) + lam_init)
    out = acc_ref[...] / l_ref[...]
    att = out[:, :TQ] - lam * out[:, TQ:]
    y = (att * lax.rsqrt(jnp.mean(att * att, axis=0, keepdims=True) + SUBLN_EPS)) * g_ref[...]
    o_ref[...] = (y * (1.0 - lam_init)).T.astype(BF16)


def _attn(q, k, vt, bias, lq1, lk1, lq2, lk2, subln_g, lam_init, batch):
    t = q.shape[0]
    nq = SEQ // TQ
    vec = lambda n: pl.BlockSpec((1, n), lambda h, b, i: (0, 0))
    tile = pl.BlockSpec((TQ, V_DIM), lambda h, b, i: (b * nq + i, h))
    return pl.pallas_call(
        functools.partial(_attn_kernel, lam_init=lam_init),
        grid=(ATT_HEADS, batch, nq),
        in_specs=[vec(QK_DIM)] * 4 + [
            pl.BlockSpec((V_DIM, 1), lambda h, b, i: (0, 0)),
            tile,
            pl.BlockSpec((SEQ, V_DIM), lambda h, b, i: (b, h)),
            pl.BlockSpec((nq, V_DIM, TQ), lambda h, b, i: (b, h, 0)),
            pl.BlockSpec((None, 2, TQ, 2 * TQ), lambda h, b, i: (h, 0, 0, 0))],
        out_specs=tile,
        out_shape=jax.ShapeDtypeStruct((t, ATT_WIDTH), BF16),
        scratch_shapes=[pltpu.VMEM((2 * TQ, V_DIM), BF16),
                        pltpu.VMEM((TQ, 2 * TQ), F32), pltpu.VMEM((TQ, 2 * TQ), F32),
                        pltpu.VMEM((1, 2 * TQ), F32), pltpu.VMEM((1, 2 * TQ), F32),
                        pltpu.VMEM((1, 2 * TQ), F32), pltpu.VMEM((1, 2 * TQ), F32),
                        pltpu.VMEM((V_DIM, 2 * TQ), F32)],
        compiler_params=pltpu.CompilerParams(dimension_semantics=("arbitrary",) * 3,
                                             vmem_limit_bytes=VMEM_LIMIT),
        name="attn",
    )(lq1, lk1, lq2, lk2, subln_g, q, k, vt, bias)


def _out_ffn_kernel(x_ref, att_ref, pc_ref, woa_ref, wob_ref, gffn_ref, wg_ref, wu_ref, wd_ref,
                    gfin_ref, o_ref, *, final):
    y = (jnp.dot(att_ref[...], woa_ref[...], preferred_element_type=F32)
         + jnp.dot(pc_ref[...], wob_ref[...], preferred_element_type=F32))
    x1 = x_ref[...] + y
    h = _rms(x1, gffn_ref[...], EPS).astype(BF16)
    gate = jnp.dot(h, wg_ref[...], preferred_element_type=F32)
    up = jnp.dot(h, wu_ref[...], preferred_element_type=F32)
    a = (jax.nn.silu(gate) * up).astype(BF16)
    x2 = x1 + jnp.dot(a, wd_ref[...], preferred_element_type=F32)
    o_ref[...] = _rms(x2, gfin_ref[...], EPS) if final else x2


def _out_ffn(x2d, att, pc, woa, wob, gffn, wg, wu, wd, gfin, final):
    t = x2d.shape[0]
    tm = TM_FFN
    const = lambda shape: pl.BlockSpec(shape, lambda i: (0, 0), pipeline_mode=pl.Buffered(1))
    row = lambda width: pl.BlockSpec((tm, width), lambda i: (i, 0))
    return pl.pallas_call(
        functools.partial(_out_ffn_kernel, final=final),
        grid=(t // tm,),
        in_specs=[row(D_MODEL), row(ATT_WIDTH), row(ATT_WIDTH),
                  const((ATT_WIDTH, D_MODEL)), const((ATT_WIDTH, D_MODEL)), const((1, D_MODEL)),
                  const((D_MODEL, D_FF)), const((D_MODEL, D_FF)), const((D_FF, D_MODEL)),
                  const((1, D_MODEL))],
        out_specs=row(D_MODEL),
        out_shape=jax.ShapeDtypeStruct((t, D_MODEL), F32),
        compiler_params=pltpu.CompilerParams(dimension_semantics=("arbitrary",),
                                             vmem_limit_bytes=VMEM_LIMIT),
        name="out_ffn",
    )(x2d, att, pc, woa, wob, gffn, wg, wu, wd, gfin)


def _block_diag(w_pool):
    groups = w_pool.shape[0]
    out = jnp.zeros((POOL_WIDTH, POOL_WIDTH), w_pool.dtype)
    for g in range(groups):
        out = out.at[g * POOL_GDIM:(g + 1) * POOL_GDIM, g * POOL_GDIM:(g + 1) * POOL_GDIM].set(w_pool[g])
    return out


def kernel(x, g_mix, w_in, lambda_q1, lambda_k1, lambda_q2, lambda_k2, subln_g, rel_bias,
           w_pool, pool_scale, conv_w, w_o, g_ffn, w_gate, w_up, w_down, g_final):
    batch, seq, d = x.shape
    assert (seq, d) == (SEQ, D_MODEL)
    depth = w_in.shape[0]
    xf = x.reshape(batch * seq, d)
    bias = _bias_tiles(rel_bias.astype(F32))
    for l in range(depth):
        lam_init = 0.8 - 0.6 * math.exp(-0.3 * l)
        q, k, vt, pc = _mix_in(xf, g_mix[l][None], w_in[l].astype(BF16), _block_diag(w_pool[l]).astype(BF16),
                               pool_scale[l][None], conv_w[l])
        att = _attn(q, k, vt, bias, lambda_q1[l][None], lambda_k1[l][None], lambda_q2[l][None],
                    lambda_k2[l][None], subln_g[l][:, None], lam_init, batch)
        wo = w_o[l].astype(BF16)
        xf = _out_ffn(xf, att, pc, wo[:ATT_WIDTH], wo[ATT_WIDTH:], g_ffn[l][None],
                      w_gate[l].astype(BF16), w_up[l].astype(BF16), w_down[l].astype(BF16),
                      g_final[None], final=(l == depth - 1))
    return xf.reshape(batch, seq, d)
```

```python
import functools
import math

import numpy as np
import jax
import jax.numpy as jnp
from jax import lax
from jax.experimental import pallas as pl
from jax.experimental.pallas import tpu as pltpu

F32 = jnp.float32
BF16 = jnp.bfloat16

D_MODEL = 1024
SEQ = 4096
ATT_HEADS = 4
QK_DIM = 64
V_DIM = 2 * QK_DIM
ATT_WIDTH = ATT_HEADS * V_DIM
POOL_WINDOWS = (2, 4, 8, 16)
POOL_WIDTH = 256
POOL_GDIM = 64
CONV_WIDTH = 256
CONV_K = 3
IN_COLS = 3 * ATT_WIDTH + POOL_WIDTH + 3 * CONV_WIDTH
D_FF = 2816
NUM_BUCKETS = 32
MAX_EXACT = 16
MAX_DISTANCE = 128
EPS = 1e-6
SUBLN_EPS = 1e-5
NEG = -1e30
LOG2E = math.log2(math.e)

HALO = 16
TQ = 512
TM_IN = TQ
TM_FFN = 512
VMEM_LIMIT = 56 * 1024 * 1024


def _bucket_thresholds():
    n = np.arange(MAX_DISTANCE)
    nf = np.maximum(n, 1).astype(np.float32)
    large = MAX_EXACT + (np.log(nf / MAX_EXACT) / math.log(MAX_DISTANCE / MAX_EXACT)
                         * (NUM_BUCKETS - MAX_EXACT)).astype(np.int32)
    bucket = np.where(n < MAX_EXACT, n, np.minimum(large, NUM_BUCKETS - 1))
    assert bucket[-1] == NUM_BUCKETS - 1 and np.all(np.diff(bucket) >= 0)
    return [int(np.argmax(bucket >= b)) for b in range(NUM_BUCKETS)]


_BUCKET_THR = _bucket_thresholds()


def _rms(x, g, eps):
    return (x * lax.rsqrt(jnp.mean(x * x, axis=-1, keepdims=True) + eps)) * g


def _bias_kernel(relb_ref, out_ref):
    h = pl.program_id(0)
    which = pl.program_id(1)
    kpos = lax.broadcasted_iota(jnp.int32, (TQ, TQ), 0)
    qpos = lax.broadcasted_iota(jnp.int32, (TQ, TQ), 1)
    dist = qpos - kpos + which * TQ
    val = jnp.full((TQ, TQ), relb_ref[0, h], F32)
    for b in range(1, NUM_BUCKETS):
        val = jnp.where(dist >= _BUCKET_THR[b], relb_ref[b, h], val)
    val = jnp.where(dist >= 0, (val - relb_ref[NUM_BUCKETS - 1, h]) * LOG2E, NEG)
    out_ref[:, :TQ] = val
    out_ref[:, TQ:] = val


def _bias_tiles(rel_bias):
    return pl.pallas_call(
        _bias_kernel,
        grid=(ATT_HEADS, 2),
        in_specs=[pl.BlockSpec(memory_space=pltpu.SMEM)],
        out_specs=pl.BlockSpec((None, None, TQ, 2 * TQ), lambda h, w: (h, w, 0, 0)),
        out_shape=jax.ShapeDtypeStruct((ATT_HEADS, 2, TQ, 2 * TQ), F32),
        name="bias_tiles",
    )(rel_bias)


def _mix_in_kernel(x_ref, g_ref, w_ref, wbd_ref, pscale_ref, convw_ref,
                   q_ref, k_ref, vt_ref, pc_ref, carry_ref):
    tm = x_ref.shape[0]
    tiles_per_seq = SEQ // tm
    seq_tile = pl.program_id(0) % tiles_per_seq

    @pl.when(seq_tile == 0)
    def _():
        carry_ref[...] = jnp.zeros_like(carry_ref)

    h = _rms(x_ref[...], g_ref[...], EPS).astype(BF16)
    o = 3 * ATT_WIDTH
    proj = jnp.dot(h, w_ref[:, o:], preferred_element_type=F32)
    p = proj[:, :POOL_WIDTH]
    gb = proj[:, POOL_WIDTH:POOL_WIDTH + CONV_WIDTH]
    gc = proj[:, POOL_WIDTH + CONV_WIDTH:POOL_WIDTH + 2 * CONV_WIDTH]
    hin = proj[:, POOL_WIDTH + 2 * CONV_WIDTH:]
    u = gc * hin

    qkv = jnp.dot(h, w_ref[:, :o], preferred_element_type=F32)
    q_ref[...] = (qkv[:, 0:ATT_WIDTH] * (QK_DIM ** -0.5 * LOG2E)).astype(BF16)
    k_ref[...] = qkv[:, ATT_WIDTH:2 * ATT_WIDTH].astype(BF16)
    vt_ref[...] = qkv[:, 2 * ATT_WIDTH:].T.astype(BF16)

    p_ext = jnp.concatenate([carry_ref[:, :POOL_WIDTH], p], axis=0)
    u_ext = jnp.concatenate([carry_ref[:, POOL_WIDTH:], u], axis=0)
    carry_ref[:, :POOL_WIDTH] = p[tm - HALO:]
    carry_ref[:, POOL_WIDTH:] = u[tm - HALO:]

    a2 = p_ext + pltpu.roll(p_ext, 1, axis=0)
    a4 = a2 + pltpu.roll(a2, 2, axis=0)
    a4_hi = a4[:, 128:]
    a8 = a4_hi + pltpu.roll(a4_hi, 4, axis=0)
    a16 = a8 + pltpu.roll(a8, 8, axis=0)
    lane = lax.broadcasted_iota(jnp.int32, (tm, 128), 1)
    low = lane < POOL_GDIM
    sums = jnp.concatenate([jnp.where(low, a2[HALO:, :128], a4[HALO:, :128]),
                            jnp.where(low, a8[HALO:], a16[HALO:])], axis=1)
    col = lax.broadcasted_iota(jnp.int32, (tm, POOL_WIDTH), 1)
    win = jnp.left_shift(2, col // POOL_GDIM)
    t1 = seq_tile * tm + lax.broadcasted_iota(jnp.int32, (tm, POOL_WIDTH), 0) + 1
    pooled = sums / jnp.minimum(t1, win).astype(F32) - p
    mixed = jnp.dot(pooled.astype(BF16), wbd_ref[...], preferred_element_type=F32) * pscale_ref[...]
    pc_ref[:, :POOL_WIDTH] = mixed.astype(BF16)

    y = (convw_ref[0:1, :] * pltpu.roll(u_ext, 2, axis=0)[HALO:]
         + convw_ref[1:2, :] * pltpu.roll(u_ext, 1, axis=0)[HALO:]
         + convw_ref[2:3, :] * u)
    pc_ref[:, POOL_WIDTH:] = (gb * y).astype(BF16)


def _layer_spec(shape, layer, grid_rank):
    index_map = {1: lambda i: (layer, 0, 0), 3: lambda h, b, i: (layer, 0, 0)}[grid_rank]
    return pl.BlockSpec((None,) + shape, index_map, pipeline_mode=pl.Buffered(1))


def _mix_in(x2d, g, w_in, wbd, pscale, convw, layer):
    t = x2d.shape[0]
    tm = TM_IN
    const = lambda shape: _layer_spec(shape, layer, 1)
    row = lambda width: pl.BlockSpec((tm, width), lambda i: (i, 0))
    return pl.pallas_call(
        _mix_in_kernel,
        grid=(t // tm,),
        in_specs=[row(D_MODEL), const((1, D_MODEL)), const((D_MODEL, IN_COLS)),
                  const((POOL_WIDTH, POOL_WIDTH)), const((1, POOL_WIDTH)), const((CONV_K, CONV_WIDTH))],
        out_specs=[row(ATT_WIDTH), row(ATT_WIDTH),
                   pl.BlockSpec((None, ATT_WIDTH, tm), lambda i: (i, 0, 0)), row(ATT_WIDTH)],
        out_shape=[jax.ShapeDtypeStruct((t, ATT_WIDTH), BF16), jax.ShapeDtypeStruct((t, ATT_WIDTH), BF16),
                   jax.ShapeDtypeStruct((t // tm, ATT_WIDTH, tm), BF16),
                   jax.ShapeDtypeStruct((t, ATT_WIDTH), BF16)],
        scratch_shapes=[pltpu.VMEM((HALO, POOL_WIDTH + CONV_WIDTH), F32)],
        compiler_params=pltpu.CompilerParams(dimension_semantics=("arbitrary",),
                                             vmem_limit_bytes=VMEM_LIMIT),
        name="mix_in",
    )(x2d, g, w_in, wbd, pscale, convw)


def _attn_kernel(lq1_ref, lk1_ref, lq2_ref, lk2_ref, g_ref, q_ref, k_ref, vt_ref, bias_ref, o_ref,
                 qs_ref, sa_ref, sb_ref, mxa_ref, mxb_ref, m_ref, l_ref, acc_ref, *, lam_init):
    i = pl.program_id(2)
    q = q_ref[...]
    lane = lax.broadcasted_iota(jnp.int32, q.shape, 1)
    zero = jnp.zeros_like(q)
    qs_ref[:TQ] = jnp.where(lane < QK_DIM, q, zero)
    qs_ref[TQ:] = jnp.where(lane >= QK_DIM, q, zero)
    m_ref[...] = jnp.full_like(m_ref, NEG)
    l_ref[...] = jnp.zeros_like(l_ref)
    acc_ref[...] = jnp.zeros_like(acc_ref)
    buf_a = (sa_ref, mxa_ref)
    buf_b = (sb_ref, mxb_ref)

    def scores(j, buf, bias=None):
        s_ref, mx_ref = buf
        kb = k_ref[pl.ds(pl.multiple_of(j * TQ, TQ), TQ), :]
        s = lax.dot_general(kb, qs_ref[...], (((1,), (1,)), ((), ())),
                            preferred_element_type=F32)
        if bias is not None:
            s = s + bias
        s_ref[...] = s
        mx_ref[...] = jnp.max(s, axis=0, keepdims=True)

    def absorb(j, buf):
        s_ref, mx_ref = buf
        m_old = m_ref[...]
        m_new = jnp.maximum(m_old, mx_ref[...])
        p = jnp.exp2(s_ref[...] - m_new)
        alpha = jnp.exp2(m_old - m_new)
        l_ref[...] = alpha * l_ref[...] + jnp.sum(p, axis=0, keepdims=True)
        acc_ref[...] = alpha * acc_ref[...] + jnp.dot(vt_ref[j], p.astype(BF16),
                                                      preferred_element_type=F32)
        m_ref[...] = m_new

    scores(i, buf_a, bias_ref[0])

    @pl.when(i == 0)
    def _():
        absorb(0, buf_a)

    @pl.when(i >= 1)
    def _():
        scores(i - 1, buf_b, bias_ref[1])
        absorb(i, buf_a)
        n_plain = i - 1

        def pair(t, carry):
            jb = i - 1 - 2 * t
            scores(jb - 1, buf_a)
            absorb(jb, buf_b)
            scores(jb - 2, buf_b)
            absorb(jb - 1, buf_a)
            return carry

        lax.fori_loop(0, n_plain // 2, pair, 0)

        @pl.when(n_plain % 2 == 1)
        def _():
            scores(0, buf_a)
            absorb(1, buf_b)
            absorb(0, buf_a)

        @pl.when(n_plain % 2 == 0)
        def _():
            absorb(0, buf_b)

    lam = (jnp.exp(jnp.sum(lq1_ref[...] * lk1_ref[...], axis=-1, keepdims=True))
           - jnp.exp(jnp.sum(lq2_ref[...] * lk2_ref[...], axis=-1, keepdims=True)) + lam_init)
    out = acc_ref[...] / l_ref[...]
    att = out[:, :TQ] - lam * out[:, TQ:]
    y = (att * lax.rsqrt(jnp.mean(att * att, axis=0, keepdims=True) + SUBLN_EPS)) * g_ref[...]
    o_ref[...] = (y * (1.0 - lam_init)).T.astype(BF16)


def _attn(q, k, vt, bias, lq1, lk1, lq2, lk2, subln_g, lam_init, batch, layer):
    t = q.shape[0]
    nq = SEQ // TQ
    tile = pl.BlockSpec((TQ, V_DIM), lambda h, b, i: (b * nq + i, h))
    return pl.pallas_call(
        functools.partial(_attn_kernel, lam_init=lam_init),
        grid=(ATT_HEADS, batch, nq),
        in_specs=[_layer_spec((1, QK_DIM), layer, 3)] * 4 + [
            _layer_spec((V_DIM, 1), layer, 3),
            tile,
            pl.BlockSpec((SEQ, V_DIM), lambda h, b, i: (b, h)),
            pl.BlockSpec((nq, V_DIM, TQ), lambda h, b, i: (b, h, 0)),
            pl.BlockSpec((None, 2, TQ, 2 * TQ), lambda h, b, i: (h, 0, 0, 0))],
        out_specs=tile,
        out_shape=jax.ShapeDtypeStruct((t, ATT_WIDTH), BF16),
        scratch_shapes=[pltpu.VMEM((2 * TQ, V_DIM), BF16),
                        pltpu.VMEM((TQ, 2 * TQ), F32), pltpu.VMEM((TQ, 2 * TQ), F32),
                        pltpu.VMEM((1, 2 * TQ), F32), pltpu.VMEM((1, 2 * TQ), F32),
                        pltpu.VMEM((1, 2 * TQ), F32), pltpu.VMEM((1, 2 * TQ), F32),
                        pltpu.VMEM((V_DIM, 2 * TQ), F32)],
        compiler_params=pltpu.CompilerParams(dimension_semantics=("arbitrary",) * 3,
                                             vmem_limit_bytes=VMEM_LIMIT),
        name="attn",
    )(lq1, lk1, lq2, lk2, subln_g, q, k, vt, bias)


def _out_ffn_kernel(x_ref, att_ref, pc_ref, wo_ref, gffn_ref, wg_ref, wu_ref, wd_ref,
                    gfin_ref, o_ref, *, final):
    y = (jnp.dot(att_ref[...], wo_ref[:ATT_WIDTH], preferred_element_type=F32)
         + jnp.dot(pc_ref[...], wo_ref[ATT_WIDTH:], preferred_element_type=F32))
    x1 = x_ref[...] + y
    h = _rms(x1, gffn_ref[...], EPS).astype(BF16)
    gate = jnp.dot(h, wg_ref[...], preferred_element_type=F32)
    up = jnp.dot(h, wu_ref[...], preferred_element_type=F32)
    a = (jax.nn.silu(gate) * up).astype(BF16)
    x2 = x1 + jnp.dot(a, wd_ref[...], preferred_element_type=F32)
    o_ref[...] = _rms(x2, gfin_ref[...], EPS) if final else x2


def _out_ffn(x2d, att, pc, wo, gffn, wg, wu, wd, gfin, layer, final):
    t = x2d.shape[0]
    tm = TM_FFN
    const = lambda shape: _layer_spec(shape, layer, 1)
    row = lambda width: pl.BlockSpec((tm, width), lambda i: (i, 0))
    return pl.pallas_call(
        functools.partial(_out_ffn_kernel, final=final),
        grid=(t // tm,),
        in_specs=[row(D_MODEL), row(ATT_WIDTH), row(ATT_WIDTH),
                  const((D_MODEL, D_MODEL)), const((1, D_MODEL)),
                  const((D_MODEL, D_FF)), const((D_MODEL, D_FF)), const((D_FF, D_MODEL)),
                  pl.BlockSpec((1, D_MODEL), lambda i: (0, 0))],
        out_specs=row(D_MODEL),
        out_shape=jax.ShapeDtypeStruct((t, D_MODEL), F32),
        compiler_params=pltpu.CompilerParams(dimension_semantics=("arbitrary",),
                                             vmem_limit_bytes=VMEM_LIMIT),
        name="out_ffn",
    )(x2d, att, pc, wo, gffn, wg, wu, wd, gfin)


def _block_diag(w_pool):
    depth, groups = w_pool.shape[:2]
    out = jnp.zeros((depth, POOL_WIDTH, POOL_WIDTH), w_pool.dtype)
    for g in range(groups):
        out = out.at[:, g * POOL_GDIM:(g + 1) * POOL_GDIM, g * POOL_GDIM:(g + 1) * POOL_GDIM].set(w_pool[:, g])
    return out


def kernel(x, g_mix, w_in, lambda_q1, lambda_k1, lambda_q2, lambda_k2, subln_g, rel_bias,
           w_pool, pool_scale, conv_w, w_o, g_ffn, w_gate, w_up, w_down, g_final):
    batch, seq, d = x.shape
    assert (seq, d) == (SEQ, D_MODEL)
    depth = w_in.shape[0]
    xf = x.reshape(batch * seq, d)
    bias = _bias_tiles(rel_bias.astype(F32))
    w_in_b, w_o_b, w_gate_b, w_up_b, w_down_b = (w.astype(BF16) for w in (w_in, w_o, w_gate, w_up, w_down))
    wbd = _block_diag(w_pool).astype(BF16)
    as_rows = lambda v: v[:, None, :]
    for l in range(depth):
        lam_init = 0.8 - 0.6 * math.exp(-0.3 * l)
        q, k, vt, pc = _mix_in(xf, as_rows(g_mix), w_in_b, wbd, as_rows(pool_scale), conv_w, l)
        att = _attn(q, k, vt, bias, as_rows(lambda_q1), as_rows(lambda_k1), as_rows(lambda_q2),
                    as_rows(lambda_k2), subln_g[:, :, None], lam_init, batch, l)
        xf = _out_ffn(xf, att, pc, w_o_b, as_rows(g_ffn), w_gate_b, w_up_b, w_down_b,
                      g_final[None], l, final=(l == depth - 1))
    return xf.reshape(batch, seq, d)
```

```python
import functools
import math

import numpy as np
import jax
import jax.numpy as jnp
from jax import lax
from jax.experimental import pallas as pl
from jax.experimental.pallas import tpu as pltpu

F32 = jnp.float32
BF16 = jnp.bfloat16

D_MODEL = 1024
SEQ = 4096
ATT_HEADS = 4
QK_DIM = 64
V_DIM = 2 * QK_DIM
ATT_WIDTH = ATT_HEADS * V_DIM
BF16_ROWS = 16
VT_ROWS = V_DIM + BF16_ROWS
POOL_WINDOWS = (2, 4, 8, 16)
POOL_WIDTH = 256
POOL_GDIM = 64
CONV_WIDTH = 256
CONV_K = 3
IN_COLS = 3 * ATT_WIDTH + POOL_WIDTH + 3 * CONV_WIDTH
D_FF = 2816
NUM_BUCKETS = 32
MAX_EXACT = 16
MAX_DISTANCE = 128
EPS = 1e-6
SUBLN_EPS = 1e-5
NEG = -1e30
LOG2E = math.log2(math.e)

HALO = 16
TQ = 512
TM_IN = TQ
TM_FFN = 512
VMEM_LIMIT = 56 * 1024 * 1024


def _bucket_thresholds():
    n = np.arange(MAX_DISTANCE)
    nf = np.maximum(n, 1).astype(np.float32)
    large = MAX_EXACT + (np.log(nf / MAX_EXACT) / math.log(MAX_DISTANCE / MAX_EXACT)
                         * (NUM_BUCKETS - MAX_EXACT)).astype(np.int32)
    bucket = np.where(n < MAX_EXACT, n, np.minimum(large, NUM_BUCKETS - 1))
    assert bucket[-1] == NUM_BUCKETS - 1 and np.all(np.diff(bucket) >= 0)
    return [int(np.argmax(bucket >= b)) for b in range(NUM_BUCKETS)]


_BUCKET_THR = _bucket_thresholds()


def _rms(x, g, eps):
    return (x * lax.rsqrt(jnp.mean(x * x, axis=-1, keepdims=True) + eps)) * g


def _bias_kernel(relb_ref, out_ref):
    h = pl.program_id(0)
    which = pl.program_id(1)
    kpos = lax.broadcasted_iota(jnp.int32, (TQ, TQ), 0)
    qpos = lax.broadcasted_iota(jnp.int32, (TQ, TQ), 1)
    dist = qpos - kpos + which * TQ
    val = jnp.full((TQ, TQ), relb_ref[0, h], F32)
    for b in range(1, NUM_BUCKETS):
        val = jnp.where(dist >= _BUCKET_THR[b], relb_ref[b, h], val)
    val = jnp.where(dist >= 0, (val - relb_ref[NUM_BUCKETS - 1, h]) * LOG2E, NEG)
    out_ref[:, :TQ] = val
    out_ref[:, TQ:] = val


def _bias_tiles(rel_bias):
    return pl.pallas_call(
        _bias_kernel,
        grid=(ATT_HEADS, 2),
        in_specs=[pl.BlockSpec(memory_space=pltpu.SMEM)],
        out_specs=pl.BlockSpec((None, None, TQ, 2 * TQ), lambda h, w: (h, w, 0, 0)),
        out_shape=jax.ShapeDtypeStruct((ATT_HEADS, 2, TQ, 2 * TQ), F32),
        name="bias_tiles",
    )(rel_bias)


def _mix_in_kernel(x_ref, g_ref, w_ref, wbd_ref, pscale_ref, convw_ref,
                   q_ref, k_ref, vt_ref, pc_ref, carry_ref):
    tm = x_ref.shape[0]
    tiles_per_seq = SEQ // tm
    seq_tile = pl.program_id(0) % tiles_per_seq

    @pl.when(seq_tile == 0)
    def _():
        carry_ref[...] = jnp.zeros_like(carry_ref)

    h = _rms(x_ref[...], g_ref[...], EPS).astype(BF16)
    o = 3 * ATT_WIDTH
    proj = jnp.dot(h, w_ref[:, o:], preferred_element_type=F32)
    p = proj[:, :POOL_WIDTH]
    gb = proj[:, POOL_WIDTH:POOL_WIDTH + CONV_WIDTH]
    gc = proj[:, POOL_WIDTH + CONV_WIDTH:POOL_WIDTH + 2 * CONV_WIDTH]
    hin = proj[:, POOL_WIDTH + 2 * CONV_WIDTH:]
    u = gc * hin

    qkv = jnp.dot(h, w_ref[:, :o], preferred_element_type=F32)
    q_ref[...] = (qkv[:, 0:ATT_WIDTH] * (QK_DIM ** -0.5 * LOG2E)).astype(BF16)
    k_ref[...] = qkv[:, ATT_WIDTH:2 * ATT_WIDTH].astype(BF16)
    vt = qkv[:, 2 * ATT_WIDTH:].T.astype(BF16)
    ones = jnp.ones((BF16_ROWS, tm), BF16)
    for hd in range(ATT_HEADS):
        vt_ref[hd * VT_ROWS:hd * VT_ROWS + V_DIM] = vt[hd * V_DIM:(hd + 1) * V_DIM]
        vt_ref[hd * VT_ROWS + V_DIM:(hd + 1) * VT_ROWS] = ones

    p_ext = jnp.concatenate([carry_ref[:, :POOL_WIDTH], p], axis=0)
    u_ext = jnp.concatenate([carry_ref[:, POOL_WIDTH:], u], axis=0)
    carry_ref[:, :POOL_WIDTH] = p[tm - HALO:]
    carry_ref[:, POOL_WIDTH:] = u[tm - HALO:]

    a2 = p_ext + pltpu.roll(p_ext, 1, axis=0)
    a4 = a2 + pltpu.roll(a2, 2, axis=0)
    a4_hi = a4[:, 128:]
    a8 = a4_hi + pltpu.roll(a4_hi, 4, axis=0)
    a16 = a8 + pltpu.roll(a8, 8, axis=0)
    lane = lax.broadcasted_iota(jnp.int32, (tm, 128), 1)
    low = lane < POOL_GDIM
    sums = jnp.concatenate([jnp.where(low, a2[HALO:, :128], a4[HALO:, :128]),
                            jnp.where(low, a8[HALO:], a16[HALO:])], axis=1)
    col = lax.broadcasted_iota(jnp.int32, (tm, POOL_WIDTH), 1)
    win = jnp.left_shift(2, col // POOL_GDIM)
    t1 = seq_tile * tm + lax.broadcasted_iota(jnp.int32, (tm, POOL_WIDTH), 0) + 1
    pooled = sums / jnp.minimum(t1, win).astype(F32) - p
    mixed = jnp.dot(pooled.astype(BF16), wbd_ref[...], preferred_element_type=F32) * pscale_ref[...]
    pc_ref[:, :POOL_WIDTH] = mixed.astype(BF16)

    y = (convw_ref[0:1, :] * pltpu.roll(u_ext, 2, axis=0)[HALO:]
         + convw_ref[1:2, :] * pltpu.roll(u_ext, 1, axis=0)[HALO:]
         + convw_ref[2:3, :] * u)
    pc_ref[:, POOL_WIDTH:] = (gb * y).astype(BF16)


def _layer_spec(shape, layer, grid_rank):
    index_map = {1: lambda i: (layer, 0, 0), 3: lambda h, b, i: (layer, 0, 0)}[grid_rank]
    return pl.BlockSpec((None,) + shape, index_map, pipeline_mode=pl.Buffered(1))


def _mix_in(x2d, g, w_in, wbd, pscale, convw, layer):
    t = x2d.shape[0]
    tm = TM_IN
    const = lambda shape: _layer_spec(shape, layer, 1)
    row = lambda width: pl.BlockSpec((tm, width), lambda i: (i, 0))
    return pl.pallas_call(
        _mix_in_kernel,
        grid=(t // tm,),
        in_specs=[row(D_MODEL), const((1, D_MODEL)), const((D_MODEL, IN_COLS)),
                  const((POOL_WIDTH, POOL_WIDTH)), const((1, POOL_WIDTH)), const((CONV_K, CONV_WIDTH))],
        out_specs=[row(ATT_WIDTH), row(ATT_WIDTH),
                   pl.BlockSpec((None, ATT_HEADS * VT_ROWS, tm), lambda i: (i, 0, 0)), row(ATT_WIDTH)],
        out_shape=[jax.ShapeDtypeStruct((t, ATT_WIDTH), BF16), jax.ShapeDtypeStruct((t, ATT_WIDTH), BF16),
                   jax.ShapeDtypeStruct((t // tm, ATT_HEADS * VT_ROWS, tm), BF16),
                   jax.ShapeDtypeStruct((t, ATT_WIDTH), BF16)],
        scratch_shapes=[pltpu.VMEM((HALO, POOL_WIDTH + CONV_WIDTH), F32)],
        compiler_params=pltpu.CompilerParams(dimension_semantics=("arbitrary",),
                                             vmem_limit_bytes=VMEM_LIMIT),
        name="mix_in",
    )(x2d, g, w_in, wbd, pscale, convw)


def _attn_kernel(lq1_ref, lk1_ref, lq2_ref, lk2_ref, g_ref, q_ref, k_ref, vt_ref, bias_ref, o_ref,
                 qs_ref, s0_ref, s1_ref, s2_ref, mx0_ref, mx1_ref, mx2_ref, p0_ref, p1_ref, p2_ref,
                 m_ref, acc_ref, *, lam_init):
    i = pl.program_id(2)
    qt = q_ref[...].astype(F32).T.astype(BF16)
    zero = jnp.zeros((QK_DIM, TQ), BF16)
    qs_ref[:QK_DIM, :TQ] = qt[:QK_DIM]
    qs_ref[:QK_DIM, TQ:] = zero
    qs_ref[QK_DIM:, :TQ] = zero
    qs_ref[QK_DIM:, TQ:] = qt[QK_DIM:]
    m_ref[...] = jnp.full_like(m_ref, NEG)
    acc_ref[...] = jnp.zeros_like(acc_ref)
    bufs = ((s0_ref, mx0_ref, p0_ref), (s1_ref, mx1_ref, p1_ref), (s2_ref, mx2_ref, p2_ref))

    def scores(j, buf, bias=None):
        s_ref, mx_ref, _ = buf
        kb = k_ref[pl.ds(pl.multiple_of(j * TQ, TQ), TQ), :]
        s = jnp.dot(kb, qs_ref[...], preferred_element_type=F32)
        if bias is not None:
            s = s + bias
        s_ref[...] = s
        mx_ref[...] = jnp.max(s, axis=0, keepdims=True)

    def absorb(j, buf):
        s_ref, mx_ref, p_ref = buf
        m_old = m_ref[...]
        m_new = jnp.maximum(m_old, mx_ref[...])
        p_ref[...] = jnp.exp2((s_ref[...] - m_new).astype(BF16))
        alpha = jnp.exp2(m_old - m_new)
        acc_ref[...] = alpha * acc_ref[...] + jnp.dot(vt_ref[j], p_ref[...], preferred_element_type=F32)
        m_ref[...] = m_new

    scores(i, bufs[0], bias_ref[0])

    @pl.when(i == 0)
    def _():
        absorb(0, bufs[0])

    @pl.when(i >= 1)
    def _():
        scores(i - 1, bufs[1], bias_ref[1])
        absorb(i, bufs[0])
        n_plain = i - 1

        def step(j, slot):
            scores(j - 1, bufs[(slot + 1) % 3])
            absorb(j, bufs[slot])

        def triple(t, carry):
            j = i - 1 - 3 * t
            step(j, 1)
            step(j - 1, 2)
            step(j - 2, 0)
            return carry

        lax.fori_loop(0, n_plain // 3, triple, 0)
        rest = n_plain % 3

        @pl.when(rest == 0)
        def _():
            absorb(0, bufs[1])

        @pl.when(rest == 1)
        def _():
            step(1, 1)
            absorb(0, bufs[2])

        @pl.when(rest == 2)
        def _():
            step(2, 1)
            step(1, 2)
            absorb(0, bufs[0])

    lam = (jnp.exp(jnp.sum(lq1_ref[...] * lk1_ref[...], axis=-1, keepdims=True))
           - jnp.exp(jnp.sum(lq2_ref[...] * lk2_ref[...], axis=-1, keepdims=True)) + lam_init)
    out = acc_ref[:V_DIM] / acc_ref[V_DIM:V_DIM + 1]
    att = out[:, :TQ] - lam * out[:, TQ:]
    y = (att * lax.rsqrt(jnp.mean(att * att, axis=0, keepdims=True) + SUBLN_EPS)) * g_ref[...]
    o_ref[...] = (y * (1.0 - lam_init)).T.astype(BF16)


def _attn(q, k, vt, bias, lq1, lk1, lq2, lk2, subln_g, lam_init, batch, layer):
    t = q.shape[0]
    nq = SEQ // TQ
    tile = pl.BlockSpec((TQ, V_DIM), lambda h, b, i: (b * nq + i, h))
    return pl.pallas_call(
        functools.partial(_attn_kernel, lam_init=lam_init),
        grid=(ATT_HEADS, batch, nq),
        in_specs=[_layer_spec((1, QK_DIM), layer, 3)] * 4 + [
            _layer_spec((V_DIM, 1), layer, 3),
            tile,
            pl.BlockSpec((SEQ, V_DIM), lambda h, b, i: (b, h)),
            pl.BlockSpec((nq, VT_ROWS, TQ), lambda h, b, i: (b, h, 0)),
            pl.BlockSpec((None, 2, TQ, 2 * TQ), lambda h, b, i: (h, 0, 0, 0))],
        out_specs=tile,
        out_shape=jax.ShapeDtypeStruct((t, ATT_WIDTH), BF16),
        scratch_shapes=[pltpu.VMEM((V_DIM, 2 * TQ), BF16)]
                       + [pltpu.VMEM((TQ, 2 * TQ), F32)] * 3 + [pltpu.VMEM((1, 2 * TQ), F32)] * 3 + [
                        pltpu.VMEM((TQ, 2 * TQ), BF16)] * 3 + [pltpu.VMEM((1, 2 * TQ), F32),
                        pltpu.VMEM((VT_ROWS, 2 * TQ), F32)],
        compiler_params=pltpu.CompilerParams(dimension_semantics=("arbitrary",) * 3,
                                             vmem_limit_bytes=VMEM_LIMIT),
        name="attn",
    )(lq1, lk1, lq2, lk2, subln_g, q, k, vt, bias)


def _out_ffn_kernel(x_ref, att_ref, pc_ref, wo_ref, gffn_ref, wg_ref, wu_ref, wd_ref,
                    gfin_ref, o_ref, *, final):
    y = (jnp.dot(att_ref[...], wo_ref[:ATT_WIDTH], preferred_element_type=F32)
         + jnp.dot(pc_ref[...], wo_ref[ATT_WIDTH:], preferred_element_type=F32))
    x1 = x_ref[...] + y
    h = _rms(x1, gffn_ref[...], EPS).astype(BF16)
    gate = jnp.dot(h, wg_ref[...], preferred_element_type=F32)
    up = jnp.dot(h, wu_ref[...], preferred_element_type=F32)
    a = (jax.nn.silu(gate) * up).astype(BF16)
    x2 = x1 + jnp.dot(a, wd_ref[...], preferred_element_type=F32)
    o_ref[...] = _rms(x2, gfin_ref[...], EPS) if final else x2


def _out_ffn(x2d, att, pc, wo, gffn, wg, wu, wd, gfin, layer, final):
    t = x2d.shape[0]
    tm = TM_FFN
    const = lambda shape: _layer_spec(shape, layer, 1)
    row = lambda width: pl.BlockSpec((tm, width), lambda i: (i, 0))
    return pl.pallas_call(
        functools.partial(_out_ffn_kernel, final=final),
        grid=(t // tm,),
        in_specs=[row(D_MODEL), row(ATT_WIDTH), row(ATT_WIDTH),
                  const((D_MODEL, D_MODEL)), const((1, D_MODEL)),
                  const((D_MODEL, D_FF)), const((D_MODEL, D_FF)), const((D_FF, D_MODEL)),
                  pl.BlockSpec((1, D_MODEL), lambda i: (0, 0))],
        out_specs=row(D_MODEL),
        out_shape=jax.ShapeDtypeStruct((t, D_MODEL), F32),
        compiler_params=pltpu.CompilerParams(dimension_semantics=("arbitrary",),
                                             vmem_limit_bytes=VMEM_LIMIT),
        name="out_ffn",
    )(x2d, att, pc, wo, gffn, wg, wu, wd, gfin)


def _block_diag(w_pool):
    depth, groups = w_pool.shape[:2]
    out = jnp.zeros((depth, POOL_WIDTH, POOL_WIDTH), w_pool.dtype)
    for g in range(groups):
        out = out.at[:, g * POOL_GDIM:(g + 1) * POOL_GDIM, g * POOL_GDIM:(g + 1) * POOL_GDIM].set(w_pool[:, g])
    return out


def kernel(x, g_mix, w_in, lambda_q1, lambda_k1, lambda_q2, lambda_k2, subln_g, rel_bias,
           w_pool, pool_scale, conv_w, w_o, g_ffn, w_gate, w_up, w_down, g_final):
    batch, seq, d = x.shape
    assert (seq, d) == (SEQ, D_MODEL)
    depth = w_in.shape[0]
    xf = x.reshape(batch * seq, d)
    bias = _bias_tiles(rel_bias.astype(F32))
    w_in_b, w_o_b, w_gate_b, w_up_b, w_down_b = (w.astype(BF16) for w in (w_in, w_o, w_gate, w_up, w_down))
    wbd = _block_diag(w_pool).astype(BF16)
    as_rows = lambda v: v[:, None, :]
    for l in range(depth):
        lam_init = 0.8 - 0.6 * math.exp(-0.3 * l)
        q, k, vt, pc = _mix_in(xf, as_rows(g_mix), w_in_b, wbd, as_rows(pool_scale), conv_w, l)
        att = _attn(q, k, vt, bias, as_rows(lambda_q1), as_rows(lambda_k1), as_rows(lambda_q2),
                    as_rows(lambda_k2), subln_g[:, :, None], lam_init, batch, l)
        xf = _out_ffn(xf, att, pc, w_o_b, as_rows(g_ffn), w_gate_b, w_up_b, w_down_b,
                      g_final[None], l, final=(l == depth - 1))
    return xf.reshape(batch, seq, d)
```

```python
import functools
import math

import numpy as np
import jax
import jax.numpy as jnp
from jax import lax
from jax.experimental import pallas as pl
from jax.experimental.pallas import tpu as pltpu

F32 = jnp.float32
BF16 = jnp.bfloat16

D_MODEL = 1024
SEQ = 4096
ATT_HEADS = 4
QK_DIM = 64
V_DIM = 2 * QK_DIM
ATT_WIDTH = ATT_HEADS * V_DIM
BF16_ROWS = 16
VT_ROWS = V_DIM + BF16_ROWS
POOL_WINDOWS = (2, 4, 8, 16)
POOL_WIDTH = 256
POOL_GDIM = 64
CONV_WIDTH = 256
CONV_K = 3
IN_COLS = 3 * ATT_WIDTH + POOL_WIDTH + 3 * CONV_WIDTH
D_FF = 2816
NUM_BUCKETS = 32
MAX_EXACT = 16
MAX_DISTANCE = 128
EPS = 1e-6
SUBLN_EPS = 1e-5
NEG = -1e30
LOG2E = math.log2(math.e)

HALO = 16
TQ = 512
HEADS_PER_STEP = 2
UNROLL = 2
TM_IN = TQ
TM_FFN = 512
VMEM_LIMIT = 56 * 1024 * 1024


def _bucket_thresholds():
    n = np.arange(MAX_DISTANCE)
    nf = np.maximum(n, 1).astype(np.float32)
    large = MAX_EXACT + (np.log(nf / MAX_EXACT) / math.log(MAX_DISTANCE / MAX_EXACT)
                         * (NUM_BUCKETS - MAX_EXACT)).astype(np.int32)
    bucket = np.where(n < MAX_EXACT, n, np.minimum(large, NUM_BUCKETS - 1))
    assert bucket[-1] == NUM_BUCKETS - 1 and np.all(np.diff(bucket) >= 0)
    return [int(np.argmax(bucket >= b)) for b in range(NUM_BUCKETS)]


_BUCKET_THR = _bucket_thresholds()


def _rms(x, g, eps):
    return (x * lax.rsqrt(jnp.mean(x * x, axis=-1, keepdims=True) + eps)) * g


def _bias_kernel(relb_ref, out_ref):
    h = pl.program_id(0)
    which = pl.program_id(1)
    kpos = lax.broadcasted_iota(jnp.int32, (TQ, TQ), 0)
    qpos = lax.broadcasted_iota(jnp.int32, (TQ, TQ), 1)
    dist = qpos - kpos + which * TQ
    val = jnp.full((TQ, TQ), relb_ref[0, h], F32)
    for b in range(1, NUM_BUCKETS):
        val = jnp.where(dist >= _BUCKET_THR[b], relb_ref[b, h], val)
    val = jnp.where(dist >= 0, (val - relb_ref[NUM_BUCKETS - 1, h]) * LOG2E, NEG)
    out_ref[:, :TQ] = val
    out_ref[:, TQ:] = val


def _bias_tiles(rel_bias):
    return pl.pallas_call(
        _bias_kernel,
        grid=(ATT_HEADS, 2),
        in_specs=[pl.BlockSpec(memory_space=pltpu.SMEM)],
        out_specs=pl.BlockSpec((None, None, TQ, 2 * TQ), lambda h, w: (h, w, 0, 0)),
        out_shape=jax.ShapeDtypeStruct((ATT_HEADS, 2, TQ, 2 * TQ), F32),
        name="bias_tiles",
    )(rel_bias)


def _mix_in_kernel(x_ref, g_ref, w_ref, wbd_ref, pscale_ref, convw_ref,
                   qs_ref, k_ref, vt_ref, pc_ref, carry_ref):
    tm = x_ref.shape[0]
    tiles_per_seq = SEQ // tm
    seq_tile = pl.program_id(0) % tiles_per_seq

    @pl.when(seq_tile == 0)
    def _():
        carry_ref[...] = jnp.zeros_like(carry_ref)

    h = _rms(x_ref[...], g_ref[...], EPS).astype(BF16)
    o = 3 * ATT_WIDTH
    proj = jnp.dot(h, w_ref[:, o:], preferred_element_type=F32)
    p = proj[:, :POOL_WIDTH]
    gb = proj[:, POOL_WIDTH:POOL_WIDTH + CONV_WIDTH]
    gc = proj[:, POOL_WIDTH + CONV_WIDTH:POOL_WIDTH + 2 * CONV_WIDTH]
    hin = proj[:, POOL_WIDTH + 2 * CONV_WIDTH:]
    u = gc * hin

    qkv = jnp.dot(h, w_ref[:, :o], preferred_element_type=F32)
    qt = (qkv[:, 0:ATT_WIDTH] * (QK_DIM ** -0.5 * LOG2E)).T.astype(BF16)
    first_map = lax.broadcasted_iota(jnp.int32, qt.shape, 0) % V_DIM < QK_DIM
    zero = jnp.zeros_like(qt)
    qs_ref[:, :tm] = jnp.where(first_map, qt, zero)
    qs_ref[:, tm:] = jnp.where(first_map, zero, qt)
    k_ref[...] = qkv[:, ATT_WIDTH:2 * ATT_WIDTH].astype(BF16)
    vt = qkv[:, 2 * ATT_WIDTH:].T.astype(BF16)
    ones = jnp.ones((BF16_ROWS, tm), BF16)
    for hd in range(ATT_HEADS):
        vt_ref[hd * VT_ROWS:hd * VT_ROWS + V_DIM] = vt[hd * V_DIM:(hd + 1) * V_DIM]
        vt_ref[hd * VT_ROWS + V_DIM:(hd + 1) * VT_ROWS] = ones

    p_ext = jnp.concatenate([carry_ref[:, :POOL_WIDTH], p], axis=0)
    u_ext = jnp.concatenate([carry_ref[:, POOL_WIDTH:], u], axis=0)
    carry_ref[:, :POOL_WIDTH] = p[tm - HALO:]
    carry_ref[:, POOL_WIDTH:] = u[tm - HALO:]

    a2 = p_ext + pltpu.roll(p_ext, 1, axis=0)
    a4 = a2 + pltpu.roll(a2, 2, axis=0)
    a4_hi = a4[:, 128:]
    a8 = a4_hi + pltpu.roll(a4_hi, 4, axis=0)
    a16 = a8 + pltpu.roll(a8, 8, axis=0)
    lane = lax.broadcasted_iota(jnp.int32, (tm, 128), 1)
    low = lane < POOL_GDIM
    sums = jnp.concatenate([jnp.where(low, a2[HALO:, :128], a4[HALO:, :128]),
                            jnp.where(low, a8[HALO:], a16[HALO:])], axis=1)
    col = lax.broadcasted_iota(jnp.int32, (tm, POOL_WIDTH), 1)
    win = jnp.left_shift(2, col // POOL_GDIM)
    t1 = seq_tile * tm + lax.broadcasted_iota(jnp.int32, (tm, POOL_WIDTH), 0) + 1
    pooled = sums / jnp.minimum(t1, win).astype(F32) - p
    mixed = jnp.dot(pooled.astype(BF16), wbd_ref[...], preferred_element_type=F32) * pscale_ref[...]
    pc_ref[:, :POOL_WIDTH] = mixed.astype(BF16)

    y = (convw_ref[0:1, :] * pltpu.roll(u_ext, 2, axis=0)[HALO:]
         + convw_ref[1:2, :] * pltpu.roll(u_ext, 1, axis=0)[HALO:]
         + convw_ref[2:3, :] * u)
    pc_ref[:, POOL_WIDTH:] = (gb * y).astype(BF16)


def _layer_spec(shape, layer, grid_rank):
    index_map = {1: lambda i: (layer, 0, 0), 3: lambda h, b, i: (layer, 0, 0)}[grid_rank]
    return pl.BlockSpec((None,) + shape, index_map, pipeline_mode=pl.Buffered(1))


def _mix_in(x2d, g, w_in, wbd, pscale, convw, layer):
    t = x2d.shape[0]
    tm = TM_IN
    const = lambda shape: _layer_spec(shape, layer, 1)
    row = lambda width: pl.BlockSpec((tm, width), lambda i: (i, 0))
    return pl.pallas_call(
        _mix_in_kernel,
        grid=(t // tm,),
        in_specs=[row(D_MODEL), const((1, D_MODEL)), const((D_MODEL, IN_COLS)),
                  const((POOL_WIDTH, POOL_WIDTH)), const((1, POOL_WIDTH)), const((CONV_K, CONV_WIDTH))],
        out_specs=[pl.BlockSpec((None, ATT_WIDTH, 2 * tm), lambda i: (i, 0, 0)), row(ATT_WIDTH),
                   pl.BlockSpec((None, ATT_HEADS * VT_ROWS, tm), lambda i: (i, 0, 0)), row(ATT_WIDTH)],
        out_shape=[jax.ShapeDtypeStruct((t // tm, ATT_WIDTH, 2 * tm), BF16),
                   jax.ShapeDtypeStruct((t, ATT_WIDTH), BF16),
                   jax.ShapeDtypeStruct((t // tm, ATT_HEADS * VT_ROWS, tm), BF16),
                   jax.ShapeDtypeStruct((t, ATT_WIDTH), BF16)],
        scratch_shapes=[pltpu.VMEM((HALO, POOL_WIDTH + CONV_WIDTH), F32)],
        compiler_params=pltpu.CompilerParams(dimension_semantics=("arbitrary",),
                                             vmem_limit_bytes=VMEM_LIMIT),
        name="mix_in",
    )(x2d, g, w_in, wbd, pscale, convw)


def _attn_kernel(lq1_ref, lk1_ref, lq2_ref, lk2_ref, g_ref, qs_ref, qsnext_ref, k_ref, vt_ref, bias_ref, o_ref,
                 *scratch, lam_init):
    i = pl.program_id(2)
    last_tile = pl.num_programs(2) - 1
    heads = range(HEADS_PER_STEP)
    per_head = [scratch[hd * 8:(hd + 1) * 8] for hd in heads]
    buf_a = [(sc[0], sc[3]) for sc in per_head]
    buf_b = [(sc[1], sc[4]) for sc in per_head]
    buf_c = [(sc[2], sc[5]) for sc in per_head]
    m_refs = [sc[6] for sc in per_head]
    acc_refs = [sc[7] for sc in per_head]
    for hd in heads:
        m_refs[hd][...] = jnp.full_like(m_refs[hd], NEG)
        acc_refs[hd][...] = jnp.zeros_like(acc_refs[hd])

    def scores(j, bufs, bias=None, queries=qs_ref):
        for hd in heads:
            s_ref, mx_ref = bufs[hd]
            kb = k_ref[pl.ds(pl.multiple_of(j * TQ, TQ), TQ), hd * V_DIM:(hd + 1) * V_DIM]
            s = jnp.dot(kb, queries[hd * V_DIM:(hd + 1) * V_DIM, :],
                        preferred_element_type=F32)
            if bias is not None:
                s = s + bias_ref[hd, bias]
            s_ref[...] = s
            mx_ref[...] = jnp.max(s, axis=0, keepdims=True)

    def absorb(j, bufs):
        for hd in heads:
            s_ref, mx_ref = bufs[hd]
            m_old = m_refs[hd][...]
            m_new = jnp.maximum(m_old, mx_ref[...])
            p = jnp.exp2(s_ref[...] - m_new).astype(BF16)
            alpha = jnp.exp2(m_old - m_new)
            acc_refs[hd][...] = alpha * acc_refs[hd][...] + jnp.dot(
                vt_ref[j, hd * VT_ROWS:(hd + 1) * VT_ROWS, :], p, preferred_element_type=F32)
            m_refs[hd][...] = m_new

    def run(j, nsteps, pending, free):
        for step in range(nsteps):
            scores(j - step - 1, free)
            absorb(j - step, pending)
            pending, free = free, pending
        return pending

    def finish(pending):
        scores(jnp.minimum(i + 1, last_tile), buf_c, 0, qsnext_ref)
        absorb(0, pending)

    @pl.when(i == 0)
    def _():
        scores(0, buf_a, 0)
        finish(buf_a)

    @pl.when(i >= 1)
    def _():
        scores(i - 1, buf_b, 1)
        absorb(i, buf_c)
        n_plain = i - 1

        def unrolled(t, carry):
            run(i - 1 - UNROLL * t, UNROLL, buf_b, buf_a)
            return carry

        lax.fori_loop(0, n_plain // UNROLL, unrolled, 0)
        rest = n_plain % UNROLL
        for r in range(UNROLL):
            @pl.when(rest == r)
            def _():
                finish(run(r, r, buf_b, buf_a))

    lam = (jnp.exp(jnp.sum(lq1_ref[...] * lk1_ref[...], axis=-1, keepdims=True))
           - jnp.exp(jnp.sum(lq2_ref[...] * lk2_ref[...], axis=-1, keepdims=True)) + lam_init)
    for hd in heads:
        out = acc_refs[hd][:V_DIM] / acc_refs[hd][V_DIM:V_DIM + 1]
        att = out[:, :TQ] - lam * out[:, TQ:]
        y = (att * lax.rsqrt(jnp.mean(att * att, axis=0, keepdims=True) + SUBLN_EPS)) * g_ref[...]
        o_ref[:, hd * V_DIM:(hd + 1) * V_DIM] = (y * (1.0 - lam_init)).T.astype(BF16)


def _attn(qs, k, vt, bias, lq1, lk1, lq2, lk2, subln_g, lam_init, batch, layer):
    t = k.shape[0]
    nq = SEQ // TQ
    hps = HEADS_PER_STEP
    queries = lambda tile_of: pl.BlockSpec((None, hps * V_DIM, 2 * TQ),
                                           lambda h, b, i: (b * nq + tile_of(i), h, 0))
    per_head_scratch = ([pltpu.VMEM((TQ, 2 * TQ), F32)] * 3 + [pltpu.VMEM((1, 2 * TQ), F32)] * 4
                        + [pltpu.VMEM((VT_ROWS, 2 * TQ), F32)])
    return pl.pallas_call(
        functools.partial(_attn_kernel, lam_init=lam_init),
        grid=(ATT_HEADS // hps, batch, nq),
        in_specs=[_layer_spec((1, QK_DIM), layer, 3)] * 4 + [
            _layer_spec((V_DIM, 1), layer, 3),
            queries(lambda i: i), queries(lambda i: jnp.minimum(i + 1, nq - 1)),
            pl.BlockSpec((SEQ, hps * V_DIM), lambda h, b, i: (b, h)),
            pl.BlockSpec((nq, hps * VT_ROWS, TQ), lambda h, b, i: (b, h, 0)),
            pl.BlockSpec((hps, 2, TQ, 2 * TQ), lambda h, b, i: (h, 0, 0, 0), pipeline_mode=pl.Buffered(1))],
        out_specs=pl.BlockSpec((TQ, hps * V_DIM), lambda h, b, i: (b * nq + i, h)),
        out_shape=jax.ShapeDtypeStruct((t, ATT_WIDTH), BF16),
        scratch_shapes=per_head_scratch * hps,
        compiler_params=pltpu.CompilerParams(dimension_semantics=("arbitrary",) * 3,
                                             vmem_limit_bytes=VMEM_LIMIT),
        name="attn",
    )(lq1, lk1, lq2, lk2, subln_g, qs, qs, k, vt, bias)


def _out_ffn_kernel(x_ref, att_ref, pc_ref, wo_ref, gffn_ref, wg_ref, wu_ref, wd_ref,
                    gfin_ref, o_ref, *, final):
    y = (jnp.dot(att_ref[...], wo_ref[:ATT_WIDTH], preferred_element_type=F32)
         + jnp.dot(pc_ref[...], wo_ref[ATT_WIDTH:], preferred_element_type=F32))
    x1 = x_ref[...] + y
    h = _rms(x1, gffn_ref[...], EPS).astype(BF16)
    gate = jnp.dot(h, wg_ref[...], preferred_element_type=F32)
    up = jnp.dot(h, wu_ref[...], preferred_element_type=F32)
    a = (jax.nn.silu(gate) * up).astype(BF16)
    x2 = x1 + jnp.dot(a, wd_ref[...], preferred_element_type=F32)
    o_ref[...] = _rms(x2, gfin_ref[...], EPS) if final else x2


def _out_ffn(x2d, att, pc, wo, gffn, wg, wu, wd, gfin, layer, final):
    t = x2d.shape[0]
    tm = TM_FFN
    const = lambda shape: _layer_spec(shape, layer, 1)
    row = lambda width: pl.BlockSpec((tm, width), lambda i: (i, 0))
    return pl.pallas_call(
        functools.partial(_out_ffn_kernel, final=final),
        grid=(t // tm,),
        in_specs=[row(D_MODEL), row(ATT_WIDTH), row(ATT_WIDTH),
                  const((D_MODEL, D_MODEL)), const((1, D_MODEL)),
                  const((D_MODEL, D_FF)), const((D_MODEL, D_FF)), const((D_FF, D_MODEL)),
                  pl.BlockSpec((1, D_MODEL), lambda i: (0, 0))],
        out_specs=row(D_MODEL),
        out_shape=jax.ShapeDtypeStruct((t, D_MODEL), F32),
        compiler_params=pltpu.CompilerParams(dimension_semantics=("arbitrary",),
                                             vmem_limit_bytes=VMEM_LIMIT),
        name="out_ffn",
    )(x2d, att, pc, wo, gffn, wg, wu, wd, gfin)


def _block_diag(w_pool):
    depth, groups = w_pool.shape[:2]
    out = jnp.zeros((depth, POOL_WIDTH, POOL_WIDTH), w_pool.dtype)
    for g in range(groups):
        out = out.at[:, g * POOL_GDIM:(g + 1) * POOL_GDIM, g * POOL_GDIM:(g + 1) * POOL_GDIM].set(w_pool[:, g])
    return out


def kernel(x, g_mix, w_in, lambda_q1, lambda_k1, lambda_q2, lambda_k2, subln_g, rel_bias,
           w_pool, pool_scale, conv_w, w_o, g_ffn, w_gate, w_up, w_down, g_final):
    batch, seq, d = x.shape
    assert (seq, d) == (SEQ, D_MODEL)
    depth = w_in.shape[0]
    xf = x.reshape(batch * seq, d)
    bias = _bias_tiles(rel_bias.astype(F32))
    w_in_b, w_o_b, w_gate_b, w_up_b, w_down_b = (w.astype(BF16) for w in (w_in, w_o, w_gate, w_up, w_down))
    wbd = _block_diag(w_pool).astype(BF16)
    as_rows = lambda v: v[:, None, :]
    for l in range(depth):
        lam_init = 0.8 - 0.6 * math.exp(-0.3 * l)
        qs, k, vt, pc = _mix_in(xf, as_rows(g_mix), w_in_b, wbd, as_rows(pool_scale), conv_w, l)
        att = _attn(qs, k, vt, bias, as_rows(lambda_q1), as_rows(lambda_k1), as_rows(lambda_q2),
                    as_rows(lambda_k2), subln_g[:, :, None], lam_init, batch, l)
        xf = _out_ffn(xf, att, pc, w_o_b, as_rows(g_ffn), w_gate_b, w_up_b, w_down_b,
                      g_final[None], l, final=(l == depth - 1))
    return xf.reshape(batch, seq, d)
```

```python
import functools
import math

import numpy as np
import jax
import jax.numpy as jnp
from jax import lax
from jax.experimental import pallas as pl
from jax.experimental.pallas import tpu as pltpu

F32 = jnp.float32
BF16 = jnp.bfloat16

D_MODEL = 1024
SEQ = 4096
ATT_HEADS = 4
QK_DIM = 64
V_DIM = 2 * QK_DIM
ATT_WIDTH = ATT_HEADS * V_DIM
BF16_ROWS = 16
VT_ROWS = V_DIM + BF16_ROWS
POOL_WINDOWS = (2, 4, 8, 16)
POOL_WIDTH = 256
POOL_GDIM = 64
CONV_WIDTH = 256
CONV_K = 3
IN_COLS = 3 * ATT_WIDTH + POOL_WIDTH + 3 * CONV_WIDTH
D_FF = 2816
NUM_BUCKETS = 32
MAX_EXACT = 16
MAX_DISTANCE = 128
EPS = 1e-6
SUBLN_EPS = 1e-5
NEG = -1e30
LOG2E = math.log2(math.e)

HALO = 16
TQ = 512
HEADS_PER_STEP = 2
UNROLL = 2
TM_IN = TQ
TM_FFN = 512
VMEM_LIMIT = 56 * 1024 * 1024


def _bucket_thresholds():
    n = np.arange(MAX_DISTANCE)
    nf = np.maximum(n, 1).astype(np.float32)
    large = MAX_EXACT + (np.log(nf / MAX_EXACT) / math.log(MAX_DISTANCE / MAX_EXACT)
                         * (NUM_BUCKETS - MAX_EXACT)).astype(np.int32)
    bucket = np.where(n < MAX_EXACT, n, np.minimum(large, NUM_BUCKETS - 1))
    assert bucket[-1] == NUM_BUCKETS - 1 and np.all(np.diff(bucket) >= 0)
    return [int(np.argmax(bucket >= b)) for b in range(NUM_BUCKETS)]


_BUCKET_THR = _bucket_thresholds()


def _rms(x, g, eps):
    return (x * lax.rsqrt(jnp.mean(x * x, axis=-1, keepdims=True) + eps)) * g


def _bias_kernel(relb_ref, out_ref):
    h = pl.program_id(0)
    which = pl.program_id(1)
    kpos = lax.broadcasted_iota(jnp.int32, (TQ, TQ), 0)
    qpos = lax.broadcasted_iota(jnp.int32, (TQ, TQ), 1)
    dist = qpos - kpos + which * TQ
    val = jnp.full((TQ, TQ), relb_ref[0, h], F32)
    for b in range(1, NUM_BUCKETS):
        val = jnp.where(dist >= _BUCKET_THR[b], relb_ref[b, h], val)
    val = jnp.where(dist >= 0, (val - relb_ref[NUM_BUCKETS - 1, h]) * LOG2E, NEG)
    out_ref[:, :TQ] = val
    out_ref[:, TQ:] = val


def _bias_tiles(rel_bias):
    return pl.pallas_call(
        _bias_kernel,
        grid=(ATT_HEADS, 2),
        in_specs=[pl.BlockSpec(memory_space=pltpu.SMEM)],
        out_specs=pl.BlockSpec((None, None, TQ, 2 * TQ), lambda h, w: (h, w, 0, 0)),
        out_shape=jax.ShapeDtypeStruct((ATT_HEADS, 2, TQ, 2 * TQ), F32),
        name="bias_tiles",
    )(rel_bias)


def _mix_in_kernel(x_ref, g_ref, w_ref, wbd_ref, pscale_ref, convw_ref,
                   qs_ref, k_ref, vt_ref, pc_ref, carry_ref):
    tm = x_ref.shape[0]
    tiles_per_seq = SEQ // tm
    seq_tile = pl.program_id(0) % tiles_per_seq

    @pl.when(seq_tile == 0)
    def _():
        carry_ref[...] = jnp.zeros_like(carry_ref)

    h = _rms(x_ref[...], g_ref[...], EPS).astype(BF16)
    o = 3 * ATT_WIDTH
    proj = jnp.dot(h, w_ref[:, o:], preferred_element_type=F32)
    p = proj[:, :POOL_WIDTH]
    gb = proj[:, POOL_WIDTH:POOL_WIDTH + CONV_WIDTH]
    gc = proj[:, POOL_WIDTH + CONV_WIDTH:POOL_WIDTH + 2 * CONV_WIDTH]
    hin = proj[:, POOL_WIDTH + 2 * CONV_WIDTH:]
    u = gc * hin

    qkv = jnp.dot(h, w_ref[:, :o], preferred_element_type=F32)
    qt = (qkv[:, 0:ATT_WIDTH] * (QK_DIM ** -0.5 * LOG2E)).T.astype(BF16)
    first_map = lax.broadcasted_iota(jnp.int32, qt.shape, 0) % V_DIM < QK_DIM
    zero = jnp.zeros_like(qt)
    qs_ref[:, :tm] = jnp.where(first_map, qt, zero)
    qs_ref[:, tm:] = jnp.where(first_map, zero, qt)
    k_ref[...] = qkv[:, ATT_WIDTH:2 * ATT_WIDTH].astype(BF16)
    vt = qkv[:, 2 * ATT_WIDTH:].T.astype(BF16)
    ones = jnp.ones((BF16_ROWS, tm), BF16)
    for hd in range(ATT_HEADS):
        vt_ref[hd * VT_ROWS:hd * VT_ROWS + V_DIM] = vt[hd * V_DIM:(hd + 1) * V_DIM]
        vt_ref[hd * VT_ROWS + V_DIM:(hd + 1) * VT_ROWS] = ones

    p_ext = jnp.concatenate([carry_ref[:, :POOL_WIDTH], p], axis=0)
    u_ext = jnp.concatenate([carry_ref[:, POOL_WIDTH:], u], axis=0)
    carry_ref[:, :POOL_WIDTH] = p[tm - HALO:]
    carry_ref[:, POOL_WIDTH:] = u[tm - HALO:]

    a2 = p_ext + pltpu.roll(p_ext, 1, axis=0)
    a4 = a2 + pltpu.roll(a2, 2, axis=0)
    a4_hi = a4[:, 128:]
    a8 = a4_hi + pltpu.roll(a4_hi, 4, axis=0)
    a16 = a8 + pltpu.roll(a8, 8, axis=0)
    lane = lax.broadcasted_iota(jnp.int32, (tm, 128), 1)
    low = lane < POOL_GDIM
    sums = jnp.concatenate([jnp.where(low, a2[HALO:, :128], a4[HALO:, :128]),
                            jnp.where(low, a8[HALO:], a16[HALO:])], axis=1)
    col = lax.broadcasted_iota(jnp.int32, (tm, POOL_WIDTH), 1)
    win = jnp.left_shift(2, col // POOL_GDIM)
    t1 = seq_tile * tm + lax.broadcasted_iota(jnp.int32, (tm, POOL_WIDTH), 0) + 1
    pooled = sums / jnp.minimum(t1, win).astype(F32) - p
    mixed = jnp.dot(pooled.astype(BF16), wbd_ref[...], preferred_element_type=F32) * pscale_ref[...]
    pc_ref[:, :POOL_WIDTH] = mixed.astype(BF16)

    y = (convw_ref[0:1, :] * pltpu.roll(u_ext, 2, axis=0)[HALO:]
         + convw_ref[1:2, :] * pltpu.roll(u_ext, 1, axis=0)[HALO:]
         + convw_ref[2:3, :] * u)
    pc_ref[:, POOL_WIDTH:] = (gb * y).astype(BF16)


def _layer_spec(shape, layer, grid_rank):
    index_map = {1: lambda i: (layer, 0, 0), 2: lambda h, b: (layer, 0, 0)}[grid_rank]
    return pl.BlockSpec((None,) + shape, index_map, pipeline_mode=pl.Buffered(1))


def _mix_in(x2d, g, w_in, wbd, pscale, convw, layer):
    t = x2d.shape[0]
    tm = TM_IN
    const = lambda shape: _layer_spec(shape, layer, 1)
    row = lambda width: pl.BlockSpec((tm, width), lambda i: (i, 0))
    return pl.pallas_call(
        _mix_in_kernel,
        grid=(t // tm,),
        in_specs=[row(D_MODEL), const((1, D_MODEL)), const((D_MODEL, IN_COLS)),
                  const((POOL_WIDTH, POOL_WIDTH)), const((1, POOL_WIDTH)), const((CONV_K, CONV_WIDTH))],
        out_specs=[pl.BlockSpec((None, ATT_WIDTH, 2 * tm), lambda i: (i, 0, 0)), row(ATT_WIDTH),
                   pl.BlockSpec((None, ATT_HEADS * VT_ROWS, tm), lambda i: (i, 0, 0)), row(ATT_WIDTH)],
        out_shape=[jax.ShapeDtypeStruct((t // tm, ATT_WIDTH, 2 * tm), BF16),
                   jax.ShapeDtypeStruct((t, ATT_WIDTH), BF16),
                   jax.ShapeDtypeStruct((t // tm, ATT_HEADS * VT_ROWS, tm), BF16),
                   jax.ShapeDtypeStruct((t, ATT_WIDTH), BF16)],
        scratch_shapes=[pltpu.VMEM((HALO, POOL_WIDTH + CONV_WIDTH), F32)],
        compiler_params=pltpu.CompilerParams(dimension_semantics=("arbitrary",),
                                             vmem_limit_bytes=VMEM_LIMIT),
        name="mix_in",
    )(x2d, g, w_in, wbd, pscale, convw)


def _attn_kernel(lq1_ref, lk1_ref, lq2_ref, lk2_ref, g_ref, qs_ref, k_ref, vt_ref, bias_ref, o_ref,
                 *scratch, lam_init):
    n_tiles = qs_ref.shape[0]
    heads = range(HEADS_PER_STEP)
    per_head = [scratch[hd * 8:(hd + 1) * 8] for hd in heads]
    buf_a = [(sc[0], sc[3]) for sc in per_head]
    buf_b = [(sc[1], sc[4]) for sc in per_head]
    buf_c = [(sc[2], sc[5]) for sc in per_head]
    m_refs = [sc[6] for sc in per_head]
    acc_refs = [sc[7] for sc in per_head]
    lam = (jnp.exp(jnp.sum(lq1_ref[...] * lk1_ref[...], axis=-1, keepdims=True))
           - jnp.exp(jnp.sum(lq2_ref[...] * lk2_ref[...], axis=-1, keepdims=True)) + lam_init)

    def tile(i, carry):
        for hd in heads:
            m_refs[hd][...] = jnp.full_like(m_refs[hd], NEG)
            acc_refs[hd][...] = jnp.zeros_like(acc_refs[hd])

        def scores(j, bufs, bias=None, qtile=i):
            for hd in heads:
                s_ref, mx_ref = bufs[hd]
                kb = k_ref[pl.ds(pl.multiple_of(j * TQ, TQ), TQ), hd * V_DIM:(hd + 1) * V_DIM]
                s = jnp.dot(kb, qs_ref[qtile, hd * V_DIM:(hd + 1) * V_DIM, :],
                            preferred_element_type=F32)
                if bias is not None:
                    s = s + bias_ref[hd, bias]
                s_ref[...] = s
                mx_ref[...] = jnp.max(s, axis=0, keepdims=True)

        def absorb(j, bufs):
            for hd in heads:
                s_ref, mx_ref = bufs[hd]
                m_old = m_refs[hd][...]
                m_new = jnp.maximum(m_old, mx_ref[...])
                p = jnp.exp2(s_ref[...] - m_new).astype(BF16)
                alpha = jnp.exp2(m_old - m_new)
                acc_refs[hd][...] = alpha * acc_refs[hd][...] + jnp.dot(
                    vt_ref[j, hd * VT_ROWS:(hd + 1) * VT_ROWS, :], p, preferred_element_type=F32)
                m_refs[hd][...] = m_new

        def run(j, nsteps, pending, free):
            for step in range(nsteps):
                scores(j - step - 1, free)
                absorb(j - step, pending)
                pending, free = free, pending
            return pending

        def finish(pending):
            nxt = jnp.minimum(i + 1, n_tiles - 1)
            scores(nxt, buf_c, 0, nxt)
            absorb(0, pending)

        @pl.when(i == 0)
        def _():
            scores(0, buf_a, 0)
            finish(buf_a)

        @pl.when(i >= 1)
        def _():
            scores(i - 1, buf_b, 1)
            absorb(i, buf_c)
            n_plain = i - 1

            def unrolled(t, c):
                run(i - 1 - UNROLL * t, UNROLL, buf_b, buf_a)
                return c

            lax.fori_loop(0, n_plain // UNROLL, unrolled, 0)
            rest = n_plain % UNROLL
            for r in range(UNROLL):
                @pl.when(rest == r)
                def _():
                    finish(run(r, r, buf_b, buf_a))

        for hd in heads:
            out = acc_refs[hd][:V_DIM] / acc_refs[hd][V_DIM:V_DIM + 1]
            att = out[:, :TQ] - lam * out[:, TQ:]
            y = (att * lax.rsqrt(jnp.mean(att * att, axis=0, keepdims=True) + SUBLN_EPS)) * g_ref[...]
            o_ref[pl.ds(pl.multiple_of(i * TQ, TQ), TQ), hd * V_DIM:(hd + 1) * V_DIM] = (
                (y * (1.0 - lam_init)).T.astype(BF16))
        return carry

    lax.fori_loop(0, n_tiles, tile, 0)


def _attn(qs, k, vt, bias, lq1, lk1, lq2, lk2, subln_g, lam_init, batch, layer):
    t = k.shape[0]
    nq = SEQ // TQ
    hps = HEADS_PER_STEP
    per_head_scratch = ([pltpu.VMEM((TQ, 2 * TQ), F32)] * 3 + [pltpu.VMEM((1, 2 * TQ), F32)] * 4
                        + [pltpu.VMEM((VT_ROWS, 2 * TQ), F32)])
    return pl.pallas_call(
        functools.partial(_attn_kernel, lam_init=lam_init),
        grid=(ATT_HEADS // hps, batch),
        in_specs=[_layer_spec((1, QK_DIM), layer, 2)] * 4 + [
            _layer_spec((V_DIM, 1), layer, 2),
            pl.BlockSpec((nq, hps * V_DIM, 2 * TQ), lambda h, b: (b, h, 0)),
            pl.BlockSpec((SEQ, hps * V_DIM), lambda h, b: (b, h)),
            pl.BlockSpec((nq, hps * VT_ROWS, TQ), lambda h, b: (b, h, 0)),
            pl.BlockSpec((hps, 2, TQ, 2 * TQ), lambda h, b: (h, 0, 0, 0), pipeline_mode=pl.Buffered(1))],
        out_specs=pl.BlockSpec((SEQ, hps * V_DIM), lambda h, b: (b, h)),
        out_shape=jax.ShapeDtypeStruct((t, ATT_WIDTH), BF16),
        scratch_shapes=per_head_scratch * hps,
        compiler_params=pltpu.CompilerParams(dimension_semantics=("arbitrary",) * 2,
                                             vmem_limit_bytes=VMEM_LIMIT),
        name="attn",
    )(lq1, lk1, lq2, lk2, subln_g, qs, k, vt, bias)


def _out_ffn_kernel(x_ref, att_ref, pc_ref, wo_ref, gffn_ref, wg_ref, wu_ref, wd_ref,
                    gfin_ref, o_ref, *, final):
    y = (jnp.dot(att_ref[...], wo_ref[:ATT_WIDTH], preferred_element_type=F32)
         + jnp.dot(pc_ref[...], wo_ref[ATT_WIDTH:], preferred_element_type=F32))
    x1 = x_ref[...] + y
    h = _rms(x1, gffn_ref[...], EPS).astype(BF16)
    gate = jnp.dot(h, wg_ref[...], preferred_element_type=F32)
    up = jnp.dot(h, wu_ref[...], preferred_element_type=F32)
    a = (jax.nn.silu(gate) * up).astype(BF16)
    x2 = x1 + jnp.dot(a, wd_ref[...], preferred_element_type=F32)
    o_ref[...] = _rms(x2, gfin_ref[...], EPS) if final else x2


def _out_ffn(x2d, att, pc, wo, gffn, wg, wu, wd, gfin, layer, final):
    t = x2d.shape[0]
    tm = TM_FFN
    const = lambda shape: _layer_spec(shape, layer, 1)
    row = lambda width: pl.BlockSpec((tm, width), lambda i: (i, 0))
    return pl.pallas_call(
        functools.partial(_out_ffn_kernel, final=final),
        grid=(t // tm,),
        in_specs=[row(D_MODEL), row(ATT_WIDTH), row(ATT_WIDTH),
                  const((D_MODEL, D_MODEL)), const((1, D_MODEL)),
                  const((D_MODEL, D_FF)), const((D_MODEL, D_FF)), const((D_FF, D_MODEL)),
                  pl.BlockSpec((1, D_MODEL), lambda i: (0, 0))],
        out_specs=row(D_MODEL),
        out_shape=jax.ShapeDtypeStruct((t, D_MODEL), F32),
        compiler_params=pltpu.CompilerParams(dimension_semantics=("arbitrary",),
                                             vmem_limit_bytes=VMEM_LIMIT),
        name="out_ffn",
    )(x2d, att, pc, wo, gffn, wg, wu, wd, gfin)


def _block_diag(w_pool):
    depth, groups = w_pool.shape[:2]
    out = jnp.zeros((depth, POOL_WIDTH, POOL_WIDTH), w_pool.dtype)
    for g in range(groups):
        out = out.at[:, g * POOL_GDIM:(g + 1) * POOL_GDIM, g * POOL_GDIM:(g + 1) * POOL_GDIM].set(w_pool[:, g])
    return out


def kernel(x, g_mix, w_in, lambda_q1, lambda_k1, lambda_q2, lambda_k2, subln_g, rel_bias,
           w_pool, pool_scale, conv_w, w_o, g_ffn, w_gate, w_up, w_down, g_final):
    batch, seq, d = x.shape
    assert (seq, d) == (SEQ, D_MODEL)
    depth = w_in.shape[0]
    xf = x.reshape(batch * seq, d)
    bias = _bias_tiles(rel_bias.astype(F32))
    w_in_b, w_o_b, w_gate_b, w_up_b, w_down_b = (w.astype(BF16) for w in (w_in, w_o, w_gate, w_up, w_down))
    wbd = _block_diag(w_pool).astype(BF16)
    as_rows = lambda v: v[:, None, :]
    for l in range(depth):
        lam_init = 0.8 - 0.6 * math.exp(-0.3 * l)
        qs, k, vt, pc = _mix_in(xf, as_rows(g_mix), w_in_b, wbd, as_rows(pool_scale), conv_w, l)
        att = _attn(qs, k, vt, bias, as_rows(lambda_q1), as_rows(lambda_k1), as_rows(lambda_q2),
                    as_rows(lambda_k2), subln_g[:, :, None], lam_init, batch, l)
        xf = _out_ffn(xf, att, pc, w_o_b, as_rows(g_ffn), w_gate_b, w_up_b, w_down_b,
                      g_final[None], l, final=(l == depth - 1))
    return xf.reshape(batch, seq, d)
```

```python
import functools
import math

import numpy as np
import jax
import jax.numpy as jnp
from jax import lax
from jax.experimental import pallas as pl
from jax.experimental.pallas import tpu as pltpu

F32 = jnp.float32
BF16 = jnp.bfloat16

D_MODEL = 1024
SEQ = 4096
ATT_HEADS = 4
QK_DIM = 64
V_DIM = 2 * QK_DIM
ATT_WIDTH = ATT_HEADS * V_DIM
BF16_ROWS = 16
VT_ROWS = V_DIM + BF16_ROWS
POOL_WINDOWS = (2, 4, 8, 16)
POOL_WIDTH = 256
POOL_GDIM = 64
CONV_WIDTH = 256
CONV_K = 3
IN_COLS = 3 * ATT_WIDTH + POOL_WIDTH + 3 * CONV_WIDTH
D_FF = 2816
NUM_BUCKETS = 32
MAX_EXACT = 16
MAX_DISTANCE = 128
EPS = 1e-6
SUBLN_EPS = 1e-5
NEG = -1e30
LOG2E = math.log2(math.e)

HALO = 16
TQ = 512
HEADS_PER_STEP = 2
UNROLL = 4
TM_IN = TQ
TM_FFN = 512
VMEM_LIMIT = 56 * 1024 * 1024


def _bucket_thresholds():
    n = np.arange(MAX_DISTANCE)
    nf = np.maximum(n, 1).astype(np.float32)
    large = MAX_EXACT + (np.log(nf / MAX_EXACT) / math.log(MAX_DISTANCE / MAX_EXACT)
                         * (NUM_BUCKETS - MAX_EXACT)).astype(np.int32)
    bucket = np.where(n < MAX_EXACT, n, np.minimum(large, NUM_BUCKETS - 1))
    assert bucket[-1] == NUM_BUCKETS - 1 and np.all(np.diff(bucket) >= 0)
    return [int(np.argmax(bucket >= b)) for b in range(NUM_BUCKETS)]


_BUCKET_THR = _bucket_thresholds()


def _rms(x, g, eps):
    return (x * lax.rsqrt(jnp.mean(x * x, axis=-1, keepdims=True) + eps)) * g


def _bias_kernel(relb_ref, out_ref):
    h = pl.program_id(0)
    which = pl.program_id(1)
    kpos = lax.broadcasted_iota(jnp.int32, (TQ, TQ), 0)
    qpos = lax.broadcasted_iota(jnp.int32, (TQ, TQ), 1)
    dist = qpos - kpos + which * TQ
    val = jnp.full((TQ, TQ), relb_ref[0, h], F32)
    for b in range(1, NUM_BUCKETS):
        val = jnp.where(dist >= _BUCKET_THR[b], relb_ref[b, h], val)
    val = jnp.where(dist >= 0, (val - relb_ref[NUM_BUCKETS - 1, h]) * LOG2E, NEG)
    out_ref[:, :TQ] = val
    out_ref[:, TQ:] = val


def _bias_tiles(rel_bias):
    return pl.pallas_call(
        _bias_kernel,
        grid=(ATT_HEADS, 2),
        in_specs=[pl.BlockSpec(memory_space=pltpu.SMEM)],
        out_specs=pl.BlockSpec((None, None, TQ, 2 * TQ), lambda h, w: (h, w, 0, 0)),
        out_shape=jax.ShapeDtypeStruct((ATT_HEADS, 2, TQ, 2 * TQ), F32),
        name="bias_tiles",
    )(rel_bias)


def _mix_in_kernel(x_ref, g_ref, w_ref, wbd_ref, pscale_ref, convw_ref,
                   qs_ref, k_ref, vt_ref, pc_ref, carry_ref):
    tm = x_ref.shape[0]
    tiles_per_seq = SEQ // tm
    seq_tile = pl.program_id(0) % tiles_per_seq

    @pl.when(seq_tile == 0)
    def _():
        carry_ref[...] = jnp.zeros_like(carry_ref)

    h = _rms(x_ref[...], g_ref[...], EPS).astype(BF16)
    o = 3 * ATT_WIDTH
    proj = jnp.dot(h, w_ref[:, o:], preferred_element_type=F32)
    p = proj[:, :POOL_WIDTH]
    gb = proj[:, POOL_WIDTH:POOL_WIDTH + CONV_WIDTH]
    gc = proj[:, POOL_WIDTH + CONV_WIDTH:POOL_WIDTH + 2 * CONV_WIDTH]
    hin = proj[:, POOL_WIDTH + 2 * CONV_WIDTH:]
    u = gc * hin

    qkv = jnp.dot(h, w_ref[:, :o], preferred_element_type=F32)
    qt = (qkv[:, 0:ATT_WIDTH] * (QK_DIM ** -0.5 * LOG2E)).T.astype(BF16)
    first_map = lax.broadcasted_iota(jnp.int32, qt.shape, 0) % V_DIM < QK_DIM
    zero = jnp.zeros_like(qt)
    qs_ref[:, :tm] = jnp.where(first_map, qt, zero)
    qs_ref[:, tm:] = jnp.where(first_map, zero, qt)
    k_ref[...] = qkv[:, ATT_WIDTH:2 * ATT_WIDTH].astype(BF16)
    vt = qkv[:, 2 * ATT_WIDTH:].T.astype(BF16)
    ones = jnp.ones((BF16_ROWS, tm), BF16)
    for hd in range(ATT_HEADS):
        vt_ref[hd * VT_ROWS:hd * VT_ROWS + V_DIM] = vt[hd * V_DIM:(hd + 1) * V_DIM]
        vt_ref[hd * VT_ROWS + V_DIM:(hd + 1) * VT_ROWS] = ones

    p_ext = jnp.concatenate([carry_ref[:, :POOL_WIDTH], p], axis=0)
    u_ext = jnp.concatenate([carry_ref[:, POOL_WIDTH:], u], axis=0)
    carry_ref[:, :POOL_WIDTH] = p[tm - HALO:]
    carry_ref[:, POOL_WIDTH:] = u[tm - HALO:]

    a2 = p_ext + pltpu.roll(p_ext, 1, axis=0)
    a4 = a2 + pltpu.roll(a2, 2, axis=0)
    a4_hi = a4[:, 128:]
    a8 = a4_hi + pltpu.roll(a4_hi, 4, axis=0)
    a16 = a8 + pltpu.roll(a8, 8, axis=0)
    lane = lax.broadcasted_iota(jnp.int32, (tm, 128), 1)
    low = lane < POOL_GDIM
    sums = jnp.concatenate([jnp.where(low, a2[HALO:, :128], a4[HALO:, :128]),
                            jnp.where(low, a8[HALO:], a16[HALO:])], axis=1)
    col = lax.broadcasted_iota(jnp.int32, (tm, POOL_WIDTH), 1)
    win = jnp.left_shift(2, col // POOL_GDIM)
    t1 = seq_tile * tm + lax.broadcasted_iota(jnp.int32, (tm, POOL_WIDTH), 0) + 1
    pooled = sums / jnp.minimum(t1, win).astype(F32) - p
    mixed = jnp.dot(pooled.astype(BF16), wbd_ref[...], preferred_element_type=F32) * pscale_ref[...]
    pc_ref[:, :POOL_WIDTH] = mixed.astype(BF16)

    y = (convw_ref[0:1, :] * pltpu.roll(u_ext, 2, axis=0)[HALO:]
         + convw_ref[1:2, :] * pltpu.roll(u_ext, 1, axis=0)[HALO:]
         + convw_ref[2:3, :] * u)
    pc_ref[:, POOL_WIDTH:] = (gb * y).astype(BF16)


def _layer_spec(shape, layer, grid_rank):
    index_map = {1: lambda i: (layer, 0, 0), 2: lambda h, b: (layer, 0, 0)}[grid_rank]
    return pl.BlockSpec((None,) + shape, index_map, pipeline_mode=pl.Buffered(1))


def _mix_in(x2d, g, w_in, wbd, pscale, convw, layer):
    t = x2d.shape[0]
    tm = TM_IN
    const = lambda shape: _layer_spec(shape, layer, 1)
    row = lambda width: pl.BlockSpec((tm, width), lambda i: (i, 0))
    return pl.pallas_call(
        _mix_in_kernel,
        grid=(t // tm,),
        in_specs=[row(D_MODEL), const((1, D_MODEL)), const((D_MODEL, IN_COLS)),
                  const((POOL_WIDTH, POOL_WIDTH)), const((1, POOL_WIDTH)), const((CONV_K, CONV_WIDTH))],
        out_specs=[pl.BlockSpec((None, ATT_WIDTH, 2 * tm), lambda i: (i, 0, 0)), row(ATT_WIDTH),
                   pl.BlockSpec((None, ATT_HEADS * VT_ROWS, tm), lambda i: (i, 0, 0)), row(ATT_WIDTH)],
        out_shape=[jax.ShapeDtypeStruct((t // tm, ATT_WIDTH, 2 * tm), BF16),
                   jax.ShapeDtypeStruct((t, ATT_WIDTH), BF16),
                   jax.ShapeDtypeStruct((t // tm, ATT_HEADS * VT_ROWS, tm), BF16),
                   jax.ShapeDtypeStruct((t, ATT_WIDTH), BF16)],
        scratch_shapes=[pltpu.VMEM((HALO, POOL_WIDTH + CONV_WIDTH), F32)],
        compiler_params=pltpu.CompilerParams(dimension_semantics=("arbitrary",),
                                             vmem_limit_bytes=VMEM_LIMIT),
        name="mix_in",
    )(x2d, g, w_in, wbd, pscale, convw)


def _attn_kernel(lq1_ref, lk1_ref, lq2_ref, lk2_ref, g_ref, qs_ref, k_ref, vt_ref, bias_ref, o_ref,
                 *scratch, lam_init):
    n_tiles = qs_ref.shape[0]
    heads = range(HEADS_PER_STEP)
    per_head = [scratch[hd * 8:(hd + 1) * 8] for hd in heads]
    buf_a = [(sc[0], sc[3]) for sc in per_head]
    buf_b = [(sc[1], sc[4]) for sc in per_head]
    buf_c = [(sc[2], sc[5]) for sc in per_head]
    m_refs = [sc[6] for sc in per_head]
    acc_refs = [sc[7] for sc in per_head]
    lam = (jnp.exp(jnp.sum(lq1_ref[...] * lk1_ref[...], axis=-1, keepdims=True))
           - jnp.exp(jnp.sum(lq2_ref[...] * lk2_ref[...], axis=-1, keepdims=True)) + lam_init)

    def tile(i, carry):
        for hd in heads:
            m_refs[hd][...] = jnp.full_like(m_refs[hd], NEG)
            acc_refs[hd][...] = jnp.zeros_like(acc_refs[hd])

        def scores(j, bufs, bias=None, qtile=i):
            for hd in heads:
                s_ref, mx_ref = bufs[hd]
                kb = k_ref[pl.ds(pl.multiple_of(j * TQ, TQ), TQ), hd * V_DIM:(hd + 1) * V_DIM]
                s = jnp.dot(kb, qs_ref[qtile, hd * V_DIM:(hd + 1) * V_DIM, :],
                            preferred_element_type=F32)
                if bias is not None:
                    s = s + bias_ref[hd, bias]
                s_ref[...] = s
                mx_ref[...] = jnp.max(s, axis=0, keepdims=True)

        def absorb(j, bufs):
            for hd in heads:
                s_ref, mx_ref = bufs[hd]
                m_old = m_refs[hd][...]
                m_new = jnp.maximum(m_old, mx_ref[...])
                p = jnp.exp2(s_ref[...] - m_new).astype(BF16)
                alpha = jnp.exp2(m_old - m_new)
                acc_refs[hd][...] = alpha * acc_refs[hd][...] + jnp.dot(
                    vt_ref[j, hd * VT_ROWS:(hd + 1) * VT_ROWS, :], p, preferred_element_type=F32)
                m_refs[hd][...] = m_new

        def run(j, nsteps, pending, free):
            for step in range(nsteps):
                scores(j - step - 1, free)
                absorb(j - step, pending)
                pending, free = free, pending
            return pending

        def finish(pending):
            nxt = jnp.minimum(i + 1, n_tiles - 1)
            scores(nxt, buf_c, 0, nxt)
            absorb(0, pending)

        @pl.when(i == 0)
        def _():
            scores(0, buf_a, 0)
            finish(buf_a)

        @pl.when(i >= 1)
        def _():
            scores(i - 1, buf_b, 1)
            absorb(i, buf_c)
            n_plain = i - 1

            def unrolled(t, c):
                run(i - 1 - UNROLL * t, UNROLL, buf_b, buf_a)
                return c

            lax.fori_loop(0, n_plain // UNROLL, unrolled, 0)
            rest = n_plain % UNROLL
            for r in range(UNROLL):
                @pl.when(rest == r)
                def _():
                    finish(run(r, r, buf_b, buf_a))

        for hd in heads:
            out = acc_refs[hd][:V_DIM] / acc_refs[hd][V_DIM:V_DIM + 1]
            att = out[:, :TQ] - lam * out[:, TQ:]
            y = (att * lax.rsqrt(jnp.mean(att * att, axis=0, keepdims=True) + SUBLN_EPS)) * g_ref[...]
            o_ref[pl.ds(pl.multiple_of(i * TQ, TQ), TQ), hd * V_DIM:(hd + 1) * V_DIM] = (
                (y * (1.0 - lam_init)).T.astype(BF16))
        return carry

    lax.fori_loop(0, n_tiles, tile, 0)


def _attn(qs, k, vt, bias, lq1, lk1, lq2, lk2, subln_g, lam_init, batch, layer):
    t = k.shape[0]
    nq = SEQ // TQ
    hps = HEADS_PER_STEP
    per_head_scratch = ([pltpu.VMEM((TQ, 2 * TQ), F32)] * 3 + [pltpu.VMEM((1, 2 * TQ), F32)] * 4
                        + [pltpu.VMEM((VT_ROWS, 2 * TQ), F32)])
    return pl.pallas_call(
        functools.partial(_attn_kernel, lam_init=lam_init),
        grid=(ATT_HEADS // hps, batch),
        in_specs=[_layer_spec((1, QK_DIM), layer, 2)] * 4 + [
            _layer_spec((V_DIM, 1), layer, 2),
            pl.BlockSpec((nq, hps * V_DIM, 2 * TQ), lambda h, b: (b, h, 0)),
            pl.BlockSpec((SEQ, hps * V_DIM), lambda h, b: (b, h)),
            pl.BlockSpec((nq, hps * VT_ROWS, TQ), lambda h, b: (b, h, 0)),
            pl.BlockSpec((hps, 2, TQ, 2 * TQ), lambda h, b: (h, 0, 0, 0), pipeline_mode=pl.Buffered(1))],
        out_specs=pl.BlockSpec((SEQ, hps * V_DIM), lambda h, b: (b, h)),
        out_shape=jax.ShapeDtypeStruct((t, ATT_WIDTH), BF16),
        scratch_shapes=per_head_scratch * hps,
        compiler_params=pltpu.CompilerParams(dimension_semantics=("arbitrary",) * 2,
                                             vmem_limit_bytes=VMEM_LIMIT),
        name="attn",
    )(lq1, lk1, lq2, lk2, subln_g, qs, k, vt, bias)


def _out_ffn_kernel(x_ref, att_ref, pc_ref, wo_ref, gffn_ref, wg_ref, wu_ref, wd_ref,
                    gfin_ref, o_ref, *, final):
    y = (jnp.dot(att_ref[...], wo_ref[:ATT_WIDTH], preferred_element_type=F32)
         + jnp.dot(pc_ref[...], wo_ref[ATT_WIDTH:], preferred_element_type=F32))
    x1 = x_ref[...] + y
    h = _rms(x1, gffn_ref[...], EPS).astype(BF16)
    gate = jnp.dot(h, wg_ref[...], preferred_element_type=F32)
    up = jnp.dot(h, wu_ref[...], preferred_element_type=F32)
    a = (jax.nn.silu(gate) * up).astype(BF16)
    x2 = x1 + jnp.dot(a, wd_ref[...], preferred_element_type=F32)
    o_ref[...] = _rms(x2, gfin_ref[...], EPS) if final else x2


def _out_ffn(x2d, att, pc, wo, gffn, wg, wu, wd, gfin, layer, final):
    t = x2d.shape[0]
    tm = TM_FFN
    const = lambda shape: _layer_spec(shape, layer, 1)
    row = lambda width: pl.BlockSpec((tm, width), lambda i: (i, 0))
    return pl.pallas_call(
        functools.partial(_out_ffn_kernel, final=final),
        grid=(t // tm,),
        in_specs=[row(D_MODEL), row(ATT_WIDTH), row(ATT_WIDTH),
                  const((D_MODEL, D_MODEL)), const((1, D_MODEL)),
                  const((D_MODEL, D_FF)), const((D_MODEL, D_FF)), const((D_FF, D_MODEL)),
                  pl.BlockSpec((1, D_MODEL), lambda i: (0, 0))],
        out_specs=row(D_MODEL),
        out_shape=jax.ShapeDtypeStruct((t, D_MODEL), F32),
        compiler_params=pltpu.CompilerParams(dimension_semantics=("arbitrary",),
                                             vmem_limit_bytes=VMEM_LIMIT),
        name="out_ffn",
    )(x2d, att, pc, wo, gffn, wg, wu, wd, gfin)


def _block_diag(w_pool):
    depth, groups = w_pool.shape[:2]
    out = jnp.zeros((depth, POOL_WIDTH, POOL_WIDTH), w_pool.dtype)
    for g in range(groups):
        out = out.at[:, g * POOL_GDIM:(g + 1) * POOL_GDIM, g * POOL_GDIM:(g + 1) * POOL_GDIM].set(w_pool[:, g])
    return out


def kernel(x, g_mix, w_in, lambda_q1, lambda_k1, lambda_q2, lambda_k2, subln_g, rel_bias,
           w_pool, pool_scale, conv_w, w_o, g_ffn, w_gate, w_up, w_down, g_final):
    batch, seq, d = x.shape
    assert (seq, d) == (SEQ, D_MODEL)
    depth = w_in.shape[0]
    xf = x.reshape(batch * seq, d)
    bias = _bias_tiles(rel_bias.astype(F32))
    w_in_b, w_o_b, w_gate_b, w_up_b, w_down_b = (w.astype(BF16) for w in (w_in, w_o, w_gate, w_up, w_down))
    wbd = _block_diag(w_pool).astype(BF16)
    as_rows = lambda v: v[:, None, :]
    for l in range(depth):
        lam_init = 0.8 - 0.6 * math.exp(-0.3 * l)
        qs, k, vt, pc = _mix_in(xf, as_rows(g_mix), w_in_b, wbd, as_rows(pool_scale), conv_w, l)
        att = _attn(qs, k, vt, bias, as_rows(lambda_q1), as_rows(lambda_k1), as_rows(lambda_q2),
                    as_rows(lambda_k2), subln_g[:, :, None], lam_init, batch, l)
        xf = _out_ffn(xf, att, pc, w_o_b, as_rows(g_ffn), w_gate_b, w_up_b, w_down_b,
                      g_final[None], l, final=(l == depth - 1))
    return xf.reshape(batch, seq, d)
```

```python
import functools
import math

import numpy as np
import jax
import jax.numpy as jnp
from jax import lax
from jax.experimental import pallas as pl
from jax.experimental.pallas import tpu as pltpu

F32 = jnp.float32
BF16 = jnp.bfloat16

D_MODEL = 1024
SEQ = 4096
ATT_HEADS = 4
QK_DIM = 64
V_DIM = 2 * QK_DIM
ATT_WIDTH = ATT_HEADS * V_DIM
BF16_ROWS = 16
VT_ROWS = V_DIM + BF16_ROWS
POOL_WINDOWS = (2, 4, 8, 16)
POOL_WIDTH = 256
POOL_GDIM = 64
CONV_WIDTH = 256
CONV_K = 3
IN_COLS = 3 * ATT_WIDTH + POOL_WIDTH + 3 * CONV_WIDTH
D_FF = 2816
NUM_BUCKETS = 32
MAX_EXACT = 16
MAX_DISTANCE = 128
EPS = 1e-6
SUBLN_EPS = 1e-5
NEG = -1e30
LOG2E = math.log2(math.e)

HALO = 16
TQ = 512
HALF = TQ // 2
HEADS_PER_STEP = 2
UNROLL = 4
TM_IN = TQ
TM_FFN = 512
VMEM_LIMIT = 56 * 1024 * 1024


def _bucket_thresholds():
    n = np.arange(MAX_DISTANCE)
    nf = np.maximum(n, 1).astype(np.float32)
    large = MAX_EXACT + (np.log(nf / MAX_EXACT) / math.log(MAX_DISTANCE / MAX_EXACT)
                         * (NUM_BUCKETS - MAX_EXACT)).astype(np.int32)
    bucket = np.where(n < MAX_EXACT, n, np.minimum(large, NUM_BUCKETS - 1))
    assert bucket[-1] == NUM_BUCKETS - 1 and np.all(np.diff(bucket) >= 0)
    return [int(np.argmax(bucket >= b)) for b in range(NUM_BUCKETS)]


_BUCKET_THR = _bucket_thresholds()


def _rms(x, g, eps):
    return (x * lax.rsqrt(jnp.mean(x * x, axis=-1, keepdims=True) + eps)) * g


def _bias_kernel(relb_ref, out_ref):
    h = pl.program_id(0)
    which = pl.program_id(1)
    kpos = lax.broadcasted_iota(jnp.int32, (TQ, TQ), 0)
    qpos = lax.broadcasted_iota(jnp.int32, (TQ, TQ), 1)
    dist = qpos - kpos + which * TQ
    val = jnp.full((TQ, TQ), relb_ref[0, h], F32)
    for b in range(1, NUM_BUCKETS):
        val = jnp.where(dist >= _BUCKET_THR[b], relb_ref[b, h], val)
    val = jnp.where(dist >= 0, (val - relb_ref[NUM_BUCKETS - 1, h]) * LOG2E, NEG)
    for chunk in range(4):
        out_ref[:, chunk * HALF:(chunk + 1) * HALF] = val[:, (chunk // 2) * HALF:(chunk // 2 + 1) * HALF]


def _bias_tiles(rel_bias):
    return pl.pallas_call(
        _bias_kernel,
        grid=(ATT_HEADS, 2),
        in_specs=[pl.BlockSpec(memory_space=pltpu.SMEM)],
        out_specs=pl.BlockSpec((None, None, TQ, 2 * TQ), lambda h, w: (h, w, 0, 0)),
        out_shape=jax.ShapeDtypeStruct((ATT_HEADS, 2, TQ, 2 * TQ), F32),
        name="bias_tiles",
    )(rel_bias)


def _mix_in_kernel(x_ref, g_ref, w_ref, wbd_ref, pscale_ref, convw_ref,
                   qs_ref, k_ref, vt_ref, pc_ref, carry_ref):
    tm = x_ref.shape[0]
    tiles_per_seq = SEQ // tm
    seq_tile = pl.program_id(0) % tiles_per_seq

    @pl.when(seq_tile == 0)
    def _():
        carry_ref[...] = jnp.zeros_like(carry_ref)

    h = _rms(x_ref[...], g_ref[...], EPS).astype(BF16)
    o = 3 * ATT_WIDTH
    proj = jnp.dot(h, w_ref[:, o:], preferred_element_type=F32)
    p = proj[:, :POOL_WIDTH]
    gb = proj[:, POOL_WIDTH:POOL_WIDTH + CONV_WIDTH]
    gc = proj[:, POOL_WIDTH + CONV_WIDTH:POOL_WIDTH + 2 * CONV_WIDTH]
    hin = proj[:, POOL_WIDTH + 2 * CONV_WIDTH:]
    u = gc * hin

    qkv = jnp.dot(h, w_ref[:, :o], preferred_element_type=F32)
    qt = (qkv[:, 0:ATT_WIDTH] * (QK_DIM ** -0.5 * LOG2E)).T.astype(BF16)
    first_map = lax.broadcasted_iota(jnp.int32, qt.shape, 0) % V_DIM < QK_DIM
    zero = jnp.zeros_like(qt)
    maps = (jnp.where(first_map, qt, zero), jnp.where(first_map, zero, qt))
    for chunk in range(4):
        qs_ref[:, chunk * HALF:(chunk + 1) * HALF] = maps[chunk % 2][:, (chunk // 2) * HALF:(chunk // 2 + 1) * HALF]
    k_ref[...] = qkv[:, ATT_WIDTH:2 * ATT_WIDTH].astype(BF16)
    vt = qkv[:, 2 * ATT_WIDTH:].T.astype(BF16)
    ones = jnp.ones((BF16_ROWS, tm), BF16)
    for hd in range(ATT_HEADS):
        vt_ref[hd * VT_ROWS:hd * VT_ROWS + V_DIM] = vt[hd * V_DIM:(hd + 1) * V_DIM]
        vt_ref[hd * VT_ROWS + V_DIM:(hd + 1) * VT_ROWS] = ones

    p_ext = jnp.concatenate([carry_ref[:, :POOL_WIDTH], p], axis=0)
    u_ext = jnp.concatenate([carry_ref[:, POOL_WIDTH:], u], axis=0)
    carry_ref[:, :POOL_WIDTH] = p[tm - HALO:]
    carry_ref[:, POOL_WIDTH:] = u[tm - HALO:]

    a2 = p_ext + pltpu.roll(p_ext, 1, axis=0)
    a4 = a2 + pltpu.roll(a2, 2, axis=0)
    a4_hi = a4[:, 128:]
    a8 = a4_hi + pltpu.roll(a4_hi, 4, axis=0)
    a16 = a8 + pltpu.roll(a8, 8, axis=0)
    lane = lax.broadcasted_iota(jnp.int32, (tm, 128), 1)
    low = lane < POOL_GDIM
    sums = jnp.concatenate([jnp.where(low, a2[HALO:, :128], a4[HALO:, :128]),
                            jnp.where(low, a8[HALO:], a16[HALO:])], axis=1)
    col = lax.broadcasted_iota(jnp.int32, (tm, POOL_WIDTH), 1)
    win = jnp.left_shift(2, col // POOL_GDIM)
    t1 = seq_tile * tm + lax.broadcasted_iota(jnp.int32, (tm, POOL_WIDTH), 0) + 1
    pooled = sums / jnp.minimum(t1, win).astype(F32) - p
    mixed = jnp.dot(pooled.astype(BF16), wbd_ref[...], preferred_element_type=F32) * pscale_ref[...]
    pc_ref[:, :POOL_WIDTH] = mixed.astype(BF16)

    y = (convw_ref[0:1, :] * pltpu.roll(u_ext, 2, axis=0)[HALO:]
         + convw_ref[1:2, :] * pltpu.roll(u_ext, 1, axis=0)[HALO:]
         + convw_ref[2:3, :] * u)
    pc_ref[:, POOL_WIDTH:] = (gb * y).astype(BF16)


def _layer_spec(shape, layer, grid_rank):
    index_map = {1: lambda i: (layer, 0, 0), 2: lambda h, b: (layer, 0, 0)}[grid_rank]
    return pl.BlockSpec((None,) + shape, index_map, pipeline_mode=pl.Buffered(1))


def _mix_in(x2d, g, w_in, wbd, pscale, convw, layer):
    t = x2d.shape[0]
    tm = TM_IN
    const = lambda shape: _layer_spec(shape, layer, 1)
    row = lambda width: pl.BlockSpec((tm, width), lambda i: (i, 0))
    return pl.pallas_call(
        _mix_in_kernel,
        grid=(t // tm,),
        in_specs=[row(D_MODEL), const((1, D_MODEL)), const((D_MODEL, IN_COLS)),
                  const((POOL_WIDTH, POOL_WIDTH)), const((1, POOL_WIDTH)), const((CONV_K, CONV_WIDTH))],
        out_specs=[pl.BlockSpec((None, ATT_WIDTH, 2 * tm), lambda i: (i, 0, 0)), row(ATT_WIDTH),
                   pl.BlockSpec((None, ATT_HEADS * VT_ROWS, tm), lambda i: (i, 0, 0)), row(ATT_WIDTH)],
        out_shape=[jax.ShapeDtypeStruct((t // tm, ATT_WIDTH, 2 * tm), BF16),
                   jax.ShapeDtypeStruct((t, ATT_WIDTH), BF16),
                   jax.ShapeDtypeStruct((t // tm, ATT_HEADS * VT_ROWS, tm), BF16),
                   jax.ShapeDtypeStruct((t, ATT_WIDTH), BF16)],
        scratch_shapes=[pltpu.VMEM((HALO, POOL_WIDTH + CONV_WIDTH), F32)],
        compiler_params=pltpu.CompilerParams(dimension_semantics=("arbitrary",),
                                             vmem_limit_bytes=VMEM_LIMIT),
        name="mix_in",
    )(x2d, g, w_in, wbd, pscale, convw)


def _attn_kernel(lq1_ref, lk1_ref, lq2_ref, lk2_ref, g_ref, qs_ref, k_ref, vt_ref, bias_ref, o_ref,
                 *scratch, lam_init):
    n_tiles = qs_ref.shape[0]
    heads = range(HEADS_PER_STEP)
    per_head = [scratch[hd * 8:(hd + 1) * 8] for hd in heads]
    buf_a = [(sc[0], sc[3]) for sc in per_head]
    buf_b = [(sc[1], sc[4]) for sc in per_head]
    buf_c = [(sc[2], sc[5]) for sc in per_head]
    m_refs = [sc[6] for sc in per_head]
    acc_refs = [sc[7] for sc in per_head]
    lam = (jnp.exp(jnp.sum(lq1_ref[...] * lk1_ref[...], axis=-1, keepdims=True))
           - jnp.exp(jnp.sum(lq2_ref[...] * lk2_ref[...], axis=-1, keepdims=True)) + lam_init)

    def tile(i, carry):
        for hd in heads:
            m_refs[hd][...] = jnp.full_like(m_refs[hd], NEG)
            acc_refs[hd][...] = jnp.zeros_like(acc_refs[hd])

        def scores(j, bufs, bias=None, qtile=i):
            for hd in heads:
                s_ref, mx_ref = bufs[hd]
                cols = slice(hd * V_DIM, (hd + 1) * V_DIM)
                q = qs_ref[qtile, cols, :]
                if bias == 0:
                    k_lo = k_ref[pl.ds(pl.multiple_of(j * TQ, TQ), HALF), cols]
                    k_hi = k_ref[pl.ds(pl.multiple_of(j * TQ + HALF, HALF), HALF), cols]
                    s_lo = jnp.dot(k_lo, q, preferred_element_type=F32) + bias_ref[hd, 0, :HALF, :]
                    s_hi = jnp.dot(k_hi, q[:, TQ:], preferred_element_type=F32) + bias_ref[hd, 0, HALF:, TQ:]
                    s_ref[:HALF, :] = s_lo
                    s_ref[HALF:, TQ:] = s_hi
                    mx_lo = jnp.max(s_lo, axis=0, keepdims=True)
                    mx_ref[:, :TQ] = mx_lo[:, :TQ]
                    mx_ref[:, TQ:] = jnp.maximum(mx_lo[:, TQ:], jnp.max(s_hi, axis=0, keepdims=True))
                    continue
                kb = k_ref[pl.ds(pl.multiple_of(j * TQ, TQ), TQ), cols]
                s = jnp.dot(kb, q, preferred_element_type=F32)
                if bias is not None:
                    s = s + bias_ref[hd, bias]
                s_ref[...] = s
                mx_ref[...] = jnp.max(s, axis=0, keepdims=True)

        def absorb(j, bufs, diagonal=False):
            for hd in heads:
                s_ref, mx_ref = bufs[hd]
                acc_ref = acc_refs[hd]
                m_old = m_refs[hd][...]
                m_new = jnp.maximum(m_old, mx_ref[...])
                alpha = jnp.exp2(m_old - m_new)
                vt = vt_ref[j, hd * VT_ROWS:(hd + 1) * VT_ROWS, :]
                if diagonal:
                    p_lo = jnp.exp2(s_ref[:HALF, :] - m_new).astype(BF16)
                    p_hi = jnp.exp2(s_ref[HALF:, TQ:] - m_new[:, TQ:]).astype(BF16)
                    pv = jnp.dot(vt[:, :HALF], p_lo, preferred_element_type=F32)
                    pv_hi = jnp.dot(vt[:, HALF:], p_hi, preferred_element_type=F32)
                    acc_ref[:, :TQ] = alpha[:, :TQ] * acc_ref[:, :TQ] + pv[:, :TQ]
                    acc_ref[:, TQ:] = alpha[:, TQ:] * acc_ref[:, TQ:] + (pv[:, TQ:] + pv_hi)
                else:
                    p = jnp.exp2(s_ref[...] - m_new).astype(BF16)
                    acc_ref[...] = alpha * acc_ref[...] + jnp.dot(vt, p, preferred_element_type=F32)
                m_refs[hd][...] = m_new

        def run(j, nsteps, pending, free):
            for step in range(nsteps):
                scores(j - step - 1, free)
                absorb(j - step, pending)
                pending, free = free, pending
            return pending

        def finish(pending, diagonal=False):
            nxt = jnp.minimum(i + 1, n_tiles - 1)
            scores(nxt, buf_c, 0, nxt)
            absorb(0, pending, diagonal)

        @pl.when(i == 0)
        def _():
            scores(0, buf_a, 0)
            finish(buf_a, diagonal=True)

        @pl.when(i >= 1)
        def _():
            scores(i - 1, buf_b, 1)
            absorb(i, buf_c, diagonal=True)
            n_plain = i - 1

            def unrolled(t, c):
                run(i - 1 - UNROLL * t, UNROLL, buf_b, buf_a)
                return c

            lax.fori_loop(0, n_plain // UNROLL, unrolled, 0)
            rest = n_plain % UNROLL
            for r in range(UNROLL):
                @pl.when(rest == r)
                def _():
                    finish(run(r, r, buf_b, buf_a))

        for hd in heads:
            out = acc_refs[hd][:V_DIM] / acc_refs[hd][V_DIM:V_DIM + 1]
            att = jnp.concatenate([out[:, 0:HALF] - lam * out[:, HALF:TQ],
                                   out[:, TQ:TQ + HALF] - lam * out[:, TQ + HALF:]], axis=1)
            y = (att * lax.rsqrt(jnp.mean(att * att, axis=0, keepdims=True) + SUBLN_EPS)) * g_ref[...]
            o_ref[pl.ds(pl.multiple_of(i * TQ, TQ), TQ), hd * V_DIM:(hd + 1) * V_DIM] = (
                (y * (1.0 - lam_init)).T.astype(BF16))
        return carry

    lax.fori_loop(0, n_tiles, tile, 0)


def _attn(qs, k, vt, bias, lq1, lk1, lq2, lk2, subln_g, lam_init, batch, layer):
    t = k.shape[0]
    nq = SEQ // TQ
    hps = HEADS_PER_STEP
    per_head_scratch = ([pltpu.VMEM((TQ, 2 * TQ), F32)] * 3 + [pltpu.VMEM((1, 2 * TQ), F32)] * 4
                        + [pltpu.VMEM((VT_ROWS, 2 * TQ), F32)])
    return pl.pallas_call(
        functools.partial(_attn_kernel, lam_init=lam_init),
        grid=(ATT_HEADS // hps, batch),
        in_specs=[_layer_spec((1, QK_DIM), layer, 2)] * 4 + [
            _layer_spec((V_DIM, 1), layer, 2),
            pl.BlockSpec((nq, hps * V_DIM, 2 * TQ), lambda h, b: (b, h, 0)),
            pl.BlockSpec((SEQ, hps * V_DIM), lambda h, b: (b, h)),
            pl.BlockSpec((nq, hps * VT_ROWS, TQ), lambda h, b: (b, h, 0)),
            pl.BlockSpec((hps, 2, TQ, 2 * TQ), lambda h, b: (h, 0, 0, 0), pipeline_mode=pl.Buffered(1))],
        out_specs=pl.BlockSpec((SEQ, hps * V_DIM), lambda h, b: (b, h)),
        out_shape=jax.ShapeDtypeStruct((t, ATT_WIDTH), BF16),
        scratch_shapes=per_head_scratch * hps,
        compiler_params=pltpu.CompilerParams(dimension_semantics=("arbitrary",) * 2,
                                             vmem_limit_bytes=VMEM_LIMIT),
        name="attn",
    )(lq1, lk1, lq2, lk2, subln_g, qs, k, vt, bias)


def _out_ffn_kernel(x_ref, att_ref, pc_ref, wo_ref, gffn_ref, wg_ref, wu_ref, wd_ref,
                    gfin_ref, o_ref, *, final):
    y = (jnp.dot(att_ref[...], wo_ref[:ATT_WIDTH], preferred_element_type=F32)
         + jnp.dot(pc_ref[...], wo_ref[ATT_WIDTH:], preferred_element_type=F32))
    x1 = x_ref[...] + y
    h = _rms(x1, gffn_ref[...], EPS).astype(BF16)
    gate = jnp.dot(h, wg_ref[...], preferred_element_type=F32)
    up = jnp.dot(h, wu_ref[...], preferred_element_type=F32)
    a = (jax.nn.silu(gate) * up).astype(BF16)
    x2 = x1 + jnp.dot(a, wd_ref[...], preferred_element_type=F32)
    o_ref[...] = _rms(x2, gfin_ref[...], EPS) if final else x2


def _out_ffn(x2d, att, pc, wo, gffn, wg, wu, wd, gfin, layer, final):
    t = x2d.shape[0]
    tm = TM_FFN
    const = lambda shape: _layer_spec(shape, layer, 1)
    row = lambda width: pl.BlockSpec((tm, width), lambda i: (i, 0))
    return pl.pallas_call(
        functools.partial(_out_ffn_kernel, final=final),
        grid=(t // tm,),
        in_specs=[row(D_MODEL), row(ATT_WIDTH), row(ATT_WIDTH),
                  const((D_MODEL, D_MODEL)), const((1, D_MODEL)),
                  const((D_MODEL, D_FF)), const((D_MODEL, D_FF)), const((D_FF, D_MODEL)),
                  pl.BlockSpec((1, D_MODEL), lambda i: (0, 0))],
        out_specs=row(D_MODEL),
        out_shape=jax.ShapeDtypeStruct((t, D_MODEL), F32),
        compiler_params=pltpu.CompilerParams(dimension_semantics=("arbitrary",),
                                             vmem_limit_bytes=VMEM_LIMIT),
        name="out_ffn",
    )(x2d, att, pc, wo, gffn, wg, wu, wd, gfin)


def _block_diag(w_pool):
    depth, groups = w_pool.shape[:2]
    out = jnp.zeros((depth, POOL_WIDTH, POOL_WIDTH), w_pool.dtype)
    for g in range(groups):
        out = out.at[:, g * POOL_GDIM:(g + 1) * POOL_GDIM, g * POOL_GDIM:(g + 1) * POOL_GDIM].set(w_pool[:, g])
    return out


def kernel(x, g_mix, w_in, lambda_q1, lambda_k1, lambda_q2, lambda_k2, subln_g, rel_bias,
           w_pool, pool_scale, conv_w, w_o, g_ffn, w_gate, w_up, w_down, g_final):
    batch, seq, d = x.shape
    assert (seq, d) == (SEQ, D_MODEL)
    depth = w_in.shape[0]
    xf = x.reshape(batch * seq, d)
    bias = _bias_tiles(rel_bias.astype(F32))
    w_in_b, w_o_b, w_gate_b, w_up_b, w_down_b = (w.astype(BF16) for w in (w_in, w_o, w_gate, w_up, w_down))
    wbd = _block_diag(w_pool).astype(BF16)
    as_rows = lambda v: v[:, None, :]
    for l in range(depth):
        lam_init = 0.8 - 0.6 * math.exp(-0.3 * l)
        qs, k, vt, pc = _mix_in(xf, as_rows(g_mix), w_in_b, wbd, as_rows(pool_scale), conv_w, l)
        att = _attn(qs, k, vt, bias, as_rows(lambda_q1), as_rows(lambda_k1), as_rows(lambda_q2),
                    as_rows(lambda_k2), subln_g[:, :, None], lam_init, batch, l)
        xf = _out_ffn(xf, att, pc, w_o_b, as_rows(g_ffn), w_gate_b, w_up_b, w_down_b,
                      g_final[None], l, final=(l == depth - 1))
    return xf.reshape(batch, seq, d)
```

```python
import functools
import math

import numpy as np
import jax
import jax.numpy as jnp
from jax import lax
from jax.experimental import pallas as pl
from jax.experimental.pallas import tpu as pltpu

F32 = jnp.float32
BF16 = jnp.bfloat16

D_MODEL = 1024
SEQ = 4096
ATT_HEADS = 4
QK_DIM = 64
V_DIM = 2 * QK_DIM
ATT_WIDTH = ATT_HEADS * V_DIM
BF16_ROWS = 16
VT_ROWS = V_DIM + BF16_ROWS
POOL_WINDOWS = (2, 4, 8, 16)
POOL_WIDTH = 256
POOL_GDIM = 64
CONV_WIDTH = 256
CONV_K = 3
IN_COLS = 3 * ATT_WIDTH + POOL_WIDTH + 3 * CONV_WIDTH
D_FF = 2816
NUM_BUCKETS = 32
MAX_EXACT = 16
MAX_DISTANCE = 128
EPS = 1e-6
SUBLN_EPS = 1e-5
NEG = -1e30
LOG2E = math.log2(math.e)

HALO = 16
TQ = 512
HALF = TQ // 2
CORNER = 128
HEADS_PER_STEP = 2
UNROLL = 4
TM_IN = TQ
TM_FFN = 512
VMEM_LIMIT = 56 * 1024 * 1024


def _bucket_thresholds():
    n = np.arange(MAX_DISTANCE)
    nf = np.maximum(n, 1).astype(np.float32)
    large = MAX_EXACT + (np.log(nf / MAX_EXACT) / math.log(MAX_DISTANCE / MAX_EXACT)
                         * (NUM_BUCKETS - MAX_EXACT)).astype(np.int32)
    bucket = np.where(n < MAX_EXACT, n, np.minimum(large, NUM_BUCKETS - 1))
    assert bucket[-1] == NUM_BUCKETS - 1 and np.all(np.diff(bucket) >= 0)
    return [int(np.argmax(bucket >= b)) for b in range(NUM_BUCKETS)]


_BUCKET_THR = _bucket_thresholds()


def _rms(x, g, eps):
    return (x * lax.rsqrt(jnp.mean(x * x, axis=-1, keepdims=True) + eps)) * g


def _bias_value(relb_ref, h, dist):
    val = jnp.full(dist.shape, relb_ref[0, h], F32)
    for b in range(1, NUM_BUCKETS):
        val = jnp.where(dist >= _BUCKET_THR[b], relb_ref[b, h], val)
    return jnp.where(dist >= 0, (val - relb_ref[NUM_BUCKETS - 1, h]) * LOG2E, NEG)


def _bias_kernel(relb_ref, diag_ref, corner_ref):
    h = pl.program_id(0)
    kpos = lax.broadcasted_iota(jnp.int32, (TQ, TQ), 0)
    qpos = lax.broadcasted_iota(jnp.int32, (TQ, TQ), 1)
    val = _bias_value(relb_ref, h, qpos - kpos)
    for chunk in range(4):
        diag_ref[:, chunk * HALF:(chunk + 1) * HALF] = val[:, (chunk // 2) * HALF:(chunk // 2 + 1) * HALF]
    kpos = lax.broadcasted_iota(jnp.int32, (CORNER, CORNER), 0) + (TQ - CORNER)
    qpos = lax.broadcasted_iota(jnp.int32, (CORNER, CORNER), 1)
    corner_ref[...] = _bias_value(relb_ref, h, TQ + qpos - kpos)


def _bias_tiles(rel_bias):
    assert CORNER >= MAX_DISTANCE and _BUCKET_THR[NUM_BUCKETS - 1] <= MAX_DISTANCE
    return pl.pallas_call(
        _bias_kernel,
        grid=(ATT_HEADS,),
        in_specs=[pl.BlockSpec(memory_space=pltpu.SMEM)],
        out_specs=[pl.BlockSpec((None, TQ, 2 * TQ), lambda h: (h, 0, 0)),
                   pl.BlockSpec((None, CORNER, CORNER), lambda h: (h, 0, 0))],
        out_shape=[jax.ShapeDtypeStruct((ATT_HEADS, TQ, 2 * TQ), F32),
                   jax.ShapeDtypeStruct((ATT_HEADS, CORNER, CORNER), F32)],
        name="bias_tiles",
    )(rel_bias)


def _mix_in_kernel(x_ref, g_ref, w_ref, wbd_ref, pscale_ref, convw_ref,
                   qs_ref, k_ref, vt_ref, pc_ref, carry_ref):
    tm = x_ref.shape[0]
    tiles_per_seq = SEQ // tm
    seq_tile = pl.program_id(0) % tiles_per_seq

    @pl.when(seq_tile == 0)
    def _():
        carry_ref[...] = jnp.zeros_like(carry_ref)

    h = _rms(x_ref[...], g_ref[...], EPS).astype(BF16)
    o = 3 * ATT_WIDTH
    proj = jnp.dot(h, w_ref[:, o:], preferred_element_type=F32)
    p = proj[:, :POOL_WIDTH]
    gb = proj[:, POOL_WIDTH:POOL_WIDTH + CONV_WIDTH]
    gc = proj[:, POOL_WIDTH + CONV_WIDTH:POOL_WIDTH + 2 * CONV_WIDTH]
    hin = proj[:, POOL_WIDTH + 2 * CONV_WIDTH:]
    u = gc * hin

    qkv = jnp.dot(h, w_ref[:, :o], preferred_element_type=F32)
    qt = (qkv[:, 0:ATT_WIDTH] * (QK_DIM ** -0.5 * LOG2E)).T.astype(BF16)
    first_map = lax.broadcasted_iota(jnp.int32, qt.shape, 0) % V_DIM < QK_DIM
    zero = jnp.zeros_like(qt)
    maps = (jnp.where(first_map, qt, zero), jnp.where(first_map, zero, qt))
    for chunk in range(4):
        qs_ref[:, chunk * HALF:(chunk + 1) * HALF] = maps[chunk % 2][:, (chunk // 2) * HALF:(chunk // 2 + 1) * HALF]
    k_ref[...] = qkv[:, ATT_WIDTH:2 * ATT_WIDTH].astype(BF16)
    vt = qkv[:, 2 * ATT_WIDTH:].T.astype(BF16)
    ones = jnp.ones((BF16_ROWS, tm), BF16)
    for hd in range(ATT_HEADS):
        vt_ref[hd * VT_ROWS:hd * VT_ROWS + V_DIM] = vt[hd * V_DIM:(hd + 1) * V_DIM]
        vt_ref[hd * VT_ROWS + V_DIM:(hd + 1) * VT_ROWS] = ones

    p_ext = jnp.concatenate([carry_ref[:, :POOL_WIDTH], p], axis=0)
    u_ext = jnp.concatenate([carry_ref[:, POOL_WIDTH:], u], axis=0)
    carry_ref[:, :POOL_WIDTH] = p[tm - HALO:]
    carry_ref[:, POOL_WIDTH:] = u[tm - HALO:]

    a2 = p_ext + pltpu.roll(p_ext, 1, axis=0)
    a4 = a2 + pltpu.roll(a2, 2, axis=0)
    a4_hi = a4[:, 128:]
    a8 = a4_hi + pltpu.roll(a4_hi, 4, axis=0)
    a16 = a8 + pltpu.roll(a8, 8, axis=0)
    lane = lax.broadcasted_iota(jnp.int32, (tm, 128), 1)
    low = lane < POOL_GDIM
    sums = jnp.concatenate([jnp.where(low, a2[HALO:, :128], a4[HALO:, :128]),
                            jnp.where(low, a8[HALO:], a16[HALO:])], axis=1)
    col = lax.broadcasted_iota(jnp.int32, (tm, POOL_WIDTH), 1)
    win = jnp.left_shift(2, col // POOL_GDIM)
    t1 = seq_tile * tm + lax.broadcasted_iota(jnp.int32, (tm, POOL_WIDTH), 0) + 1
    pooled = sums / jnp.minimum(t1, win).astype(F32) - p
    mixed = jnp.dot(pooled.astype(BF16), wbd_ref[...], preferred_element_type=F32) * pscale_ref[...]
    pc_ref[:, :POOL_WIDTH] = mixed.astype(BF16)

    y = (convw_ref[0:1, :] * pltpu.roll(u_ext, 2, axis=0)[HALO:]
         + convw_ref[1:2, :] * pltpu.roll(u_ext, 1, axis=0)[HALO:]
         + convw_ref[2:3, :] * u)
    pc_ref[:, POOL_WIDTH:] = (gb * y).astype(BF16)


def _layer_spec(shape, layer, grid_rank):
    index_map = {1: lambda i: (layer, 0, 0), 2: lambda h, b: (layer, 0, 0)}[grid_rank]
    return pl.BlockSpec((None,) + shape, index_map, pipeline_mode=pl.Buffered(1))


def _mix_in(x2d, g, w_in, wbd, pscale, convw, layer):
    t = x2d.shape[0]
    tm = TM_IN
    const = lambda shape: _layer_spec(shape, layer, 1)
    row = lambda width: pl.BlockSpec((tm, width), lambda i: (i, 0))
    return pl.pallas_call(
        _mix_in_kernel,
        grid=(t // tm,),
        in_specs=[row(D_MODEL), const((1, D_MODEL)), const((D_MODEL, IN_COLS)),
                  const((POOL_WIDTH, POOL_WIDTH)), const((1, POOL_WIDTH)), const((CONV_K, CONV_WIDTH))],
        out_specs=[pl.BlockSpec((None, ATT_WIDTH, 2 * tm), lambda i: (i, 0, 0)), row(ATT_WIDTH),
                   pl.BlockSpec((None, ATT_HEADS * VT_ROWS, tm), lambda i: (i, 0, 0)), row(ATT_WIDTH)],
        out_shape=[jax.ShapeDtypeStruct((t // tm, ATT_WIDTH, 2 * tm), BF16),
                   jax.ShapeDtypeStruct((t, ATT_WIDTH), BF16),
                   jax.ShapeDtypeStruct((t // tm, ATT_HEADS * VT_ROWS, tm), BF16),
                   jax.ShapeDtypeStruct((t, ATT_WIDTH), BF16)],
        scratch_shapes=[pltpu.VMEM((HALO, POOL_WIDTH + CONV_WIDTH), F32)],
        compiler_params=pltpu.CompilerParams(dimension_semantics=("arbitrary",),
                                             vmem_limit_bytes=VMEM_LIMIT),
        name="mix_in",
    )(x2d, g, w_in, wbd, pscale, convw)


def _attn_kernel(lq1_ref, lk1_ref, lq2_ref, lk2_ref, g_ref, qs_ref, k_ref, vt_ref, bias_ref, corner_ref, o_ref,
                 *scratch, lam_init):
    n_tiles = qs_ref.shape[0]
    heads = range(HEADS_PER_STEP)
    per_head = [scratch[hd * 8:(hd + 1) * 8] for hd in heads]
    buf_a = [(sc[0], sc[3]) for sc in per_head]
    buf_b = [(sc[1], sc[4]) for sc in per_head]
    buf_c = [(sc[2], sc[5]) for sc in per_head]
    m_refs = [sc[6] for sc in per_head]
    acc_refs = [sc[7] for sc in per_head]
    lam = (jnp.exp(jnp.sum(lq1_ref[...] * lk1_ref[...], axis=-1, keepdims=True))
           - jnp.exp(jnp.sum(lq2_ref[...] * lk2_ref[...], axis=-1, keepdims=True)) + lam_init)

    def tile(i, carry):
        for hd in heads:
            m_refs[hd][...] = jnp.full_like(m_refs[hd], NEG)
            acc_refs[hd][...] = jnp.zeros_like(acc_refs[hd])

        def scores(j, bufs, bias=None, qtile=i):
            for hd in heads:
                s_ref, mx_ref = bufs[hd]
                cols = slice(hd * V_DIM, (hd + 1) * V_DIM)
                q = qs_ref[qtile, cols, :]
                if bias == 0:
                    k_lo = k_ref[pl.ds(pl.multiple_of(j * TQ, TQ), HALF), cols]
                    k_hi = k_ref[pl.ds(pl.multiple_of(j * TQ + HALF, HALF), HALF), cols]
                    s_lo = jnp.dot(k_lo, q, preferred_element_type=F32) + bias_ref[hd, :HALF, :]
                    s_hi = jnp.dot(k_hi, q[:, TQ:], preferred_element_type=F32) + bias_ref[hd, HALF:, TQ:]
                    s_ref[:HALF, :] = s_lo
                    s_ref[HALF:, TQ:] = s_hi
                    mx_lo = jnp.max(s_lo, axis=0, keepdims=True)
                    mx_ref[:, :TQ] = mx_lo[:, :TQ]
                    mx_ref[:, TQ:] = jnp.maximum(mx_lo[:, TQ:], jnp.max(s_hi, axis=0, keepdims=True))
                    continue
                kb = k_ref[pl.ds(pl.multiple_of(j * TQ, TQ), TQ), cols]
                s = jnp.dot(kb, q, preferred_element_type=F32)
                if bias == 1:
                    top, bottom = s[:TQ - CORNER], s[TQ - CORNER:]
                    corner = corner_ref[hd]
                    bottom = jnp.concatenate(
                        [bottom[:, :CORNER] + corner, bottom[:, CORNER:HALF],
                         bottom[:, HALF:HALF + CORNER] + corner, bottom[:, HALF + CORNER:]], axis=1)
                    s_ref[:TQ - CORNER] = top
                    s_ref[TQ - CORNER:] = bottom
                    mx_ref[...] = jnp.maximum(jnp.max(top, axis=0, keepdims=True),
                                              jnp.max(bottom, axis=0, keepdims=True))
                    continue
                s_ref[...] = s
                mx_ref[...] = jnp.max(s, axis=0, keepdims=True)

        def absorb(j, bufs, diagonal=False):
            for hd in heads:
                s_ref, mx_ref = bufs[hd]
                acc_ref = acc_refs[hd]
                m_old = m_refs[hd][...]
                m_new = jnp.maximum(m_old, mx_ref[...])
                alpha = jnp.exp2(m_old - m_new)
                vt = vt_ref[j, hd * VT_ROWS:(hd + 1) * VT_ROWS, :]
                if diagonal:
                    p_lo = jnp.exp2(s_ref[:HALF, :] - m_new).astype(BF16)
                    p_hi = jnp.exp2(s_ref[HALF:, TQ:] - m_new[:, TQ:]).astype(BF16)
                    pv = jnp.dot(vt[:, :HALF], p_lo, preferred_element_type=F32)
                    pv_hi = jnp.dot(vt[:, HALF:], p_hi, preferred_element_type=F32)
                    acc_ref[:, :TQ] = alpha[:, :TQ] * acc_ref[:, :TQ] + pv[:, :TQ]
                    acc_ref[:, TQ:] = alpha[:, TQ:] * acc_ref[:, TQ:] + (pv[:, TQ:] + pv_hi)
                else:
                    p = jnp.exp2(s_ref[...] - m_new).astype(BF16)
                    acc_ref[...] = alpha * acc_ref[...] + jnp.dot(vt, p, preferred_element_type=F32)
                m_refs[hd][...] = m_new

        def run(j, nsteps, pending, free):
            for step in range(nsteps):
                scores(j - step - 1, free)
                absorb(j - step, pending)
                pending, free = free, pending
            return pending

        def finish(pending, diagonal=False):
            nxt = jnp.minimum(i + 1, n_tiles - 1)
            scores(nxt, buf_c, 0, nxt)
            absorb(0, pending, diagonal)

        @pl.when(i == 0)
        def _():
            scores(0, buf_a, 0)
            finish(buf_a, diagonal=True)

        @pl.when(i >= 1)
        def _():
            scores(i - 1, buf_b, 1)
            absorb(i, buf_c, diagonal=True)
            n_plain = i - 1

            def unrolled(t, c):
                run(i - 1 - UNROLL * t, UNROLL, buf_b, buf_a)
                return c

            lax.fori_loop(0, n_plain // UNROLL, unrolled, 0)
            rest = n_plain % UNROLL
            for r in range(UNROLL):
                @pl.when(rest == r)
                def _():
                    finish(run(r, r, buf_b, buf_a))

        for hd in heads:
            out = acc_refs[hd][:V_DIM] / acc_refs[hd][V_DIM:V_DIM + 1]
            att = jnp.concatenate([out[:, 0:HALF] - lam * out[:, HALF:TQ],
                                   out[:, TQ:TQ + HALF] - lam * out[:, TQ + HALF:]], axis=1)
            y = (att * lax.rsqrt(jnp.mean(att * att, axis=0, keepdims=True) + SUBLN_EPS)) * g_ref[...]
            o_ref[pl.ds(pl.multiple_of(i * TQ, TQ), TQ), hd * V_DIM:(hd + 1) * V_DIM] = (
                (y * (1.0 - lam_init)).T.astype(BF16))
        return carry

    lax.fori_loop(0, n_tiles, tile, 0)


def _attn(qs, k, vt, bias, lq1, lk1, lq2, lk2, subln_g, lam_init, batch, layer):
    t = k.shape[0]
    nq = SEQ // TQ
    hps = HEADS_PER_STEP
    per_head_scratch = ([pltpu.VMEM((TQ, 2 * TQ), F32)] * 3 + [pltpu.VMEM((1, 2 * TQ), F32)] * 4
                        + [pltpu.VMEM((VT_ROWS, 2 * TQ), F32)])
    return pl.pallas_call(
        functools.partial(_attn_kernel, lam_init=lam_init),
        grid=(ATT_HEADS // hps, batch),
        in_specs=[_layer_spec((1, QK_DIM), layer, 2)] * 4 + [
            _layer_spec((V_DIM, 1), layer, 2),
            pl.BlockSpec((nq, hps * V_DIM, 2 * TQ), lambda h, b: (b, h, 0)),
            pl.BlockSpec((SEQ, hps * V_DIM), lambda h, b: (b, h)),
            pl.BlockSpec((nq, hps * VT_ROWS, TQ), lambda h, b: (b, h, 0)),
            pl.BlockSpec((hps, TQ, 2 * TQ), lambda h, b: (h, 0, 0), pipeline_mode=pl.Buffered(1)),
            pl.BlockSpec((hps, CORNER, CORNER), lambda h, b: (h, 0, 0), pipeline_mode=pl.Buffered(1))],
        out_specs=pl.BlockSpec((SEQ, hps * V_DIM), lambda h, b: (b, h)),
        out_shape=jax.ShapeDtypeStruct((t, ATT_WIDTH), BF16),
        scratch_shapes=per_head_scratch * hps,
        compiler_params=pltpu.CompilerParams(dimension_semantics=("arbitrary",) * 2,
                                             vmem_limit_bytes=VMEM_LIMIT),
        name="attn",
    )(lq1, lk1, lq2, lk2, subln_g, qs, k, vt, *bias)


def _out_ffn_kernel(x_ref, att_ref, pc_ref, wo_ref, gffn_ref, wg_ref, wu_ref, wd_ref,
                    gfin_ref, o_ref, *, final):
    y = (jnp.dot(att_ref[...], wo_ref[:ATT_WIDTH], preferred_element_type=F32)
         + jnp.dot(pc_ref[...], wo_ref[ATT_WIDTH:], preferred_element_type=F32))
    x1 = x_ref[...] + y
    h = _rms(x1, gffn_ref[...], EPS).astype(BF16)
    gate = jnp.dot(h, wg_ref[...], preferred_element_type=F32)
    up = jnp.dot(h, wu_ref[...], preferred_element_type=F32)
    a = (jax.nn.silu(gate) * up).astype(BF16)
    x2 = x1 + jnp.dot(a, wd_ref[...], preferred_element_type=F32)
    o_ref[...] = _rms(x2, gfin_ref[...], EPS) if final else x2


def _out_ffn(x2d, att, pc, wo, gffn, wg, wu, wd, gfin, layer, final):
    t = x2d.shape[0]
    tm = TM_FFN
    const = lambda shape: _layer_spec(shape, layer, 1)
    row = lambda width: pl.BlockSpec((tm, width), lambda i: (i, 0))
    return pl.pallas_call(
        functools.partial(_out_ffn_kernel, final=final),
        grid=(t // tm,),
        in_specs=[row(D_MODEL), row(ATT_WIDTH), row(ATT_WIDTH),
                  const((D_MODEL, D_MODEL)), const((1, D_MODEL)),
                  const((D_MODEL, D_FF)), const((D_MODEL, D_FF)), const((D_FF, D_MODEL)),
                  pl.BlockSpec((1, D_MODEL), lambda i: (0, 0))],
        out_specs=row(D_MODEL),
        out_shape=jax.ShapeDtypeStruct((t, D_MODEL), F32),
        compiler_params=pltpu.CompilerParams(dimension_semantics=("arbitrary",),
                                             vmem_limit_bytes=VMEM_LIMIT),
        name="out_ffn",
    )(x2d, att, pc, wo, gffn, wg, wu, wd, gfin)


def _block_diag(w_pool):
    depth, groups = w_pool.shape[:2]
    out = jnp.zeros((depth, POOL_WIDTH, POOL_WIDTH), w_pool.dtype)
    for g in range(groups):
        out = out.at[:, g * POOL_GDIM:(g + 1) * POOL_GDIM, g * POOL_GDIM:(g + 1) * POOL_GDIM].set(w_pool[:, g])
    return out


def kernel(x, g_mix, w_in, lambda_q1, lambda_k1, lambda_q2, lambda_k2, subln_g, rel_bias,
           w_pool, pool_scale, conv_w, w_o, g_ffn, w_gate, w_up, w_down, g_final):
    batch, seq, d = x.shape
    assert (seq, d) == (SEQ, D_MODEL)
    depth = w_in.shape[0]
    xf = x.reshape(batch * seq, d)
    bias = _bias_tiles(rel_bias.astype(F32))
    w_in_b, w_o_b, w_gate_b, w_up_b, w_down_b = (w.astype(BF16) for w in (w_in, w_o, w_gate, w_up, w_down))
    wbd = _block_diag(w_pool).astype(BF16)
    as_rows = lambda v: v[:, None, :]
    for l in range(depth):
        lam_init = 0.8 - 0.6 * math.exp(-0.3 * l)
        qs, k, vt, pc = _mix_in(xf, as_rows(g_mix), w_in_b, wbd, as_rows(pool_scale), conv_w, l)
        att = _attn(qs, k, vt, bias, as_rows(lambda_q1), as_rows(lambda_k1), as_rows(lambda_q2),
                    as_rows(lambda_k2), subln_g[:, :, None], lam_init, batch, l)
        xf = _out_ffn(xf, att, pc, w_o_b, as_rows(g_ffn), w_gate_b, w_up_b, w_down_b,
                      g_final[None], l, final=(l == depth - 1))
    return xf.reshape(batch, seq, d)
```

```python
import functools
import math

import numpy as np
import jax
import jax.numpy as jnp
from jax import lax
from jax.experimental import pallas as pl
from jax.experimental.pallas import tpu as pltpu

F32 = jnp.float32
BF16 = jnp.bfloat16

D_MODEL = 1024
SEQ = 4096
ATT_HEADS = 4
QK_DIM = 64
V_DIM = 2 * QK_DIM
ATT_WIDTH = ATT_HEADS * V_DIM
BF16_ROWS = 16
VT_ROWS = V_DIM + BF16_ROWS
POOL_WINDOWS = (2, 4, 8, 16)
POOL_WIDTH = 256
POOL_GDIM = 64
CONV_WIDTH = 256
CONV_K = 3
IN_COLS = 3 * ATT_WIDTH + POOL_WIDTH + 3 * CONV_WIDTH
D_FF = 2816
NUM_BUCKETS = 32
MAX_EXACT = 16
MAX_DISTANCE = 128
EPS = 1e-6
SUBLN_EPS = 1e-5
NEG = -1e30
LOG2E = math.log2(math.e)

HALO = 16
TQ = 512
HALF = TQ // 2
CORNER = 128
PLAIN, BELOW_DIAGONAL, DIAGONAL = range(3)
HEADS_PER_STEP = 2
UNROLL = 4
TM_IN = TQ
TM_FFN = 512
VMEM_LIMIT = 56 * 1024 * 1024


def _bucket_thresholds():
    n = np.arange(MAX_DISTANCE)
    nf = np.maximum(n, 1).astype(np.float32)
    large = MAX_EXACT + (np.log(nf / MAX_EXACT) / math.log(MAX_DISTANCE / MAX_EXACT)
                         * (NUM_BUCKETS - MAX_EXACT)).astype(np.int32)
    bucket = np.where(n < MAX_EXACT, n, np.minimum(large, NUM_BUCKETS - 1))
    assert bucket[-1] == NUM_BUCKETS - 1 and np.all(np.diff(bucket) >= 0)
    return [int(np.argmax(bucket >= b)) for b in range(NUM_BUCKETS)]


_BUCKET_THR = _bucket_thresholds()


def _rms(x, g, eps):
    return (x * lax.rsqrt(jnp.mean(x * x, axis=-1, keepdims=True) + eps)) * g


def _bias_value(relb_ref, h, dist):
    val = jnp.full(dist.shape, relb_ref[0, h], F32)
    for b in range(1, NUM_BUCKETS):
        val = jnp.where(dist >= _BUCKET_THR[b], relb_ref[b, h], val)
    return jnp.where(dist >= 0, (val - relb_ref[NUM_BUCKETS - 1, h]) * LOG2E, NEG)


def _bias_kernel(relb_ref, diag_ref, corner_ref):
    h = pl.program_id(0)
    kpos = lax.broadcasted_iota(jnp.int32, (TQ, TQ), 0)
    qpos = lax.broadcasted_iota(jnp.int32, (TQ, TQ), 1)
    val = _bias_value(relb_ref, h, qpos - kpos)
    for chunk in range(4):
        diag_ref[:, chunk * HALF:(chunk + 1) * HALF] = val[:, (chunk // 2) * HALF:(chunk // 2 + 1) * HALF]
    kpos = lax.broadcasted_iota(jnp.int32, (CORNER, CORNER), 0) + (TQ - CORNER)
    qpos = lax.broadcasted_iota(jnp.int32, (CORNER, CORNER), 1)
    corner_ref[...] = _bias_value(relb_ref, h, TQ + qpos - kpos)


def _bias_tiles(rel_bias):
    assert CORNER >= MAX_DISTANCE and _BUCKET_THR[NUM_BUCKETS - 1] <= MAX_DISTANCE
    return pl.pallas_call(
        _bias_kernel,
        grid=(ATT_HEADS,),
        in_specs=[pl.BlockSpec(memory_space=pltpu.SMEM)],
        out_specs=[pl.BlockSpec((None, TQ, 2 * TQ), lambda h: (h, 0, 0)),
                   pl.BlockSpec((None, CORNER, CORNER), lambda h: (h, 0, 0))],
        out_shape=[jax.ShapeDtypeStruct((ATT_HEADS, TQ, 2 * TQ), F32),
                   jax.ShapeDtypeStruct((ATT_HEADS, CORNER, CORNER), F32)],
        name="bias_tiles",
    )(rel_bias)


def _mix_in_kernel(x_ref, g_ref, w_ref, wbd_ref, pscale_ref, convw_ref,
                   qs_ref, k_ref, vt_ref, pc_ref, carry_ref):
    tm = x_ref.shape[0]
    tiles_per_seq = SEQ // tm
    seq_tile = pl.program_id(0) % tiles_per_seq

    @pl.when(seq_tile == 0)
    def _():
        carry_ref[...] = jnp.zeros_like(carry_ref)

    h = _rms(x_ref[...], g_ref[...], EPS).astype(BF16)
    o = 3 * ATT_WIDTH
    proj = jnp.dot(h, w_ref[:, o:], preferred_element_type=F32)
    p = proj[:, :POOL_WIDTH]
    gb = proj[:, POOL_WIDTH:POOL_WIDTH + CONV_WIDTH]
    gc = proj[:, POOL_WIDTH + CONV_WIDTH:POOL_WIDTH + 2 * CONV_WIDTH]
    hin = proj[:, POOL_WIDTH + 2 * CONV_WIDTH:]
    u = gc * hin

    qkv = jnp.dot(h, w_ref[:, :o], preferred_element_type=F32)
    qt = (qkv[:, 0:ATT_WIDTH] * (QK_DIM ** -0.5 * LOG2E)).T.astype(BF16)
    first_map = lax.broadcasted_iota(jnp.int32, qt.shape, 0) % V_DIM < QK_DIM
    zero = jnp.zeros_like(qt)
    maps = (jnp.where(first_map, qt, zero), jnp.where(first_map, zero, qt))
    for chunk in range(4):
        qs_ref[:, chunk * HALF:(chunk + 1) * HALF] = maps[chunk % 2][:, (chunk // 2) * HALF:(chunk // 2 + 1) * HALF]
    k_ref[...] = qkv[:, ATT_WIDTH:2 * ATT_WIDTH].astype(BF16)
    vt = qkv[:, 2 * ATT_WIDTH:].T.astype(BF16)
    ones = jnp.ones((BF16_ROWS, tm), BF16)
    for hd in range(ATT_HEADS):
        vt_ref[hd * VT_ROWS:hd * VT_ROWS + V_DIM] = vt[hd * V_DIM:(hd + 1) * V_DIM]
        vt_ref[hd * VT_ROWS + V_DIM:(hd + 1) * VT_ROWS] = ones

    p_ext = jnp.concatenate([carry_ref[:, :POOL_WIDTH], p], axis=0)
    u_ext = jnp.concatenate([carry_ref[:, POOL_WIDTH:], u], axis=0)
    carry_ref[:, :POOL_WIDTH] = p[tm - HALO:]
    carry_ref[:, POOL_WIDTH:] = u[tm - HALO:]

    a2 = p_ext + pltpu.roll(p_ext, 1, axis=0)
    a4 = a2 + pltpu.roll(a2, 2, axis=0)
    a4_hi = a4[:, 128:]
    a8 = a4_hi + pltpu.roll(a4_hi, 4, axis=0)
    a16 = a8 + pltpu.roll(a8, 8, axis=0)
    lane = lax.broadcasted_iota(jnp.int32, (tm, 128), 1)
    low = lane < POOL_GDIM
    sums = jnp.concatenate([jnp.where(low, a2[HALO:, :128], a4[HALO:, :128]),
                            jnp.where(low, a8[HALO:], a16[HALO:])], axis=1)
    col = lax.broadcasted_iota(jnp.int32, (tm, POOL_WIDTH), 1)
    win = jnp.left_shift(2, col // POOL_GDIM)
    t1 = seq_tile * tm + lax.broadcasted_iota(jnp.int32, (tm, POOL_WIDTH), 0) + 1
    pooled = sums / jnp.minimum(t1, win).astype(F32) - p
    mixed = jnp.dot(pooled.astype(BF16), wbd_ref[...], preferred_element_type=F32) * pscale_ref[...]
    pc_ref[:, :POOL_WIDTH] = mixed.astype(BF16)

    y = (convw_ref[0:1, :] * pltpu.roll(u_ext, 2, axis=0)[HALO:]
         + convw_ref[1:2, :] * pltpu.roll(u_ext, 1, axis=0)[HALO:]
         + convw_ref[2:3, :] * u)
    pc_ref[:, POOL_WIDTH:] = (gb * y).astype(BF16)


def _layer_spec(shape, layer, grid_rank):
    index_map = {1: lambda i: (layer, 0, 0), 2: lambda h, b: (layer, 0, 0)}[grid_rank]
    return pl.BlockSpec((None,) + shape, index_map, pipeline_mode=pl.Buffered(1))


def _mix_in(x2d, g, w_in, wbd, pscale, convw, layer):
    t = x2d.shape[0]
    tm = TM_IN
    const = lambda shape: _layer_spec(shape, layer, 1)
    row = lambda width: pl.BlockSpec((tm, width), lambda i: (i, 0))
    return pl.pallas_call(
        _mix_in_kernel,
        grid=(t // tm,),
        in_specs=[row(D_MODEL), const((1, D_MODEL)), const((D_MODEL, IN_COLS)),
                  const((POOL_WIDTH, POOL_WIDTH)), const((1, POOL_WIDTH)), const((CONV_K, CONV_WIDTH))],
        out_specs=[pl.BlockSpec((None, ATT_WIDTH, 2 * tm), lambda i: (i, 0, 0)), row(ATT_WIDTH),
                   pl.BlockSpec((None, ATT_HEADS * VT_ROWS, tm), lambda i: (i, 0, 0)), row(ATT_WIDTH)],
        out_shape=[jax.ShapeDtypeStruct((t // tm, ATT_WIDTH, 2 * tm), BF16),
                   jax.ShapeDtypeStruct((t, ATT_WIDTH), BF16),
                   jax.ShapeDtypeStruct((t // tm, ATT_HEADS * VT_ROWS, tm), BF16),
                   jax.ShapeDtypeStruct((t, ATT_WIDTH), BF16)],
        scratch_shapes=[pltpu.VMEM((HALO, POOL_WIDTH + CONV_WIDTH), F32)],
        compiler_params=pltpu.CompilerParams(dimension_semantics=("arbitrary",),
                                             vmem_limit_bytes=VMEM_LIMIT),
        name="mix_in",
    )(x2d, g, w_in, wbd, pscale, convw)


def _attn_kernel(lq1_ref, lk1_ref, lq2_ref, lk2_ref, g_ref, qs_ref, k_ref, vt_ref, bias_ref, corner_ref, o_ref,
                 *scratch, lam_init):
    n_tiles = qs_ref.shape[0]
    heads = range(HEADS_PER_STEP)
    per_head = [scratch[hd * 8:(hd + 1) * 8] for hd in heads]
    buf_a = [(sc[0], sc[3]) for sc in per_head]
    buf_b = [(sc[1], sc[4]) for sc in per_head]
    buf_c = [(sc[2], sc[5]) for sc in per_head]
    m_refs = [sc[6] for sc in per_head]
    acc_refs = [sc[7] for sc in per_head]
    lam = (jnp.exp(jnp.sum(lq1_ref[...] * lk1_ref[...], axis=-1, keepdims=True))
           - jnp.exp(jnp.sum(lq2_ref[...] * lk2_ref[...], axis=-1, keepdims=True)) + lam_init)

    def tile(i, carry):
        for hd in heads:
            m_refs[hd][...] = jnp.full_like(m_refs[hd], NEG)
            acc_refs[hd][...] = jnp.zeros_like(acc_refs[hd])

        def key_rows(kind, c):
            return HALF if (kind == DIAGONAL and c == 0) else TQ

        def scores_chunk(hd, c, j, bufs, kind, qtile):
            s_ref, mx_ref = bufs[hd]
            cols = slice(hd * V_DIM, (hd + 1) * V_DIM)
            cc = slice(c * TQ, (c + 1) * TQ)
            nk = key_rows(kind, c)
            kb = k_ref[pl.ds(pl.multiple_of(j * TQ, TQ), nk), cols]
            s = jnp.dot(kb, qs_ref[qtile, cols, cc], preferred_element_type=F32)
            if kind == DIAGONAL:
                s = s + bias_ref[hd, :nk, cc]
            if kind == BELOW_DIAGONAL and c == 0:
                top, bottom = s[:TQ - CORNER], s[TQ - CORNER:]
                corner = corner_ref[hd]
                bottom = jnp.concatenate(
                    [bottom[:, :CORNER] + corner, bottom[:, CORNER:HALF],
                     bottom[:, HALF:HALF + CORNER] + corner, bottom[:, HALF + CORNER:]], axis=1)
                s_ref[:TQ - CORNER, cc] = top
                s_ref[TQ - CORNER:, cc] = bottom
                mx_ref[:, cc] = jnp.maximum(jnp.max(top, axis=0, keepdims=True),
                                            jnp.max(bottom, axis=0, keepdims=True))
                return
            s_ref[:nk, cc] = s
            mx_ref[:, cc] = jnp.max(s, axis=0, keepdims=True)

        def absorb_chunk(hd, c, j, bufs, kind):
            s_ref, mx_ref = bufs[hd]
            cc = slice(c * TQ, (c + 1) * TQ)
            nk = key_rows(kind, c)
            m_old = m_refs[hd][:, cc]
            m_new = jnp.maximum(m_old, mx_ref[:, cc])
            p = jnp.exp2(s_ref[:nk, cc] - m_new).astype(BF16)
            alpha = jnp.exp2(m_old - m_new)
            acc_refs[hd][:, cc] = alpha * acc_refs[hd][:, cc] + jnp.dot(
                vt_ref[j, hd * VT_ROWS:(hd + 1) * VT_ROWS, :nk], p, preferred_element_type=F32)
            m_refs[hd][:, cc] = m_new

        def fused(score=None, absorb=None):
            for hd in heads:
                for c in range(2):
                    if score is not None:
                        scores_chunk(hd, c, *score)
                    if absorb is not None:
                        absorb_chunk(hd, c, *absorb)

        def run(j, nsteps, pending, free):
            for step in range(nsteps):
                fused(score=(j - step - 1, free, PLAIN, i), absorb=(j - step, pending, PLAIN))
                pending, free = free, pending
            return pending

        def finish(pending, kind=PLAIN):
            nxt = jnp.minimum(i + 1, n_tiles - 1)
            fused(score=(nxt, buf_c, DIAGONAL, nxt), absorb=(0, pending, kind))

        @pl.when(i == 0)
        def _():
            fused(score=(0, buf_a, DIAGONAL, i))
            finish(buf_a, DIAGONAL)

        @pl.when(i >= 1)
        def _():
            fused(score=(i - 1, buf_b, BELOW_DIAGONAL, i), absorb=(i, buf_c, DIAGONAL))
            n_plain = i - 1

            def unrolled(t, c):
                run(i - 1 - UNROLL * t, UNROLL, buf_b, buf_a)
                return c

            lax.fori_loop(0, n_plain // UNROLL, unrolled, 0)
            rest = n_plain % UNROLL
            for r in range(UNROLL):
                @pl.when(rest == r)
                def _():
                    finish(run(r, r, buf_b, buf_a))

        for hd in heads:
            out = acc_refs[hd][:V_DIM] / acc_refs[hd][V_DIM:V_DIM + 1]
            att = jnp.concatenate([out[:, 0:HALF] - lam * out[:, HALF:TQ],
                                   out[:, TQ:TQ + HALF] - lam * out[:, TQ + HALF:]], axis=1)
            y = (att * lax.rsqrt(jnp.mean(att * att, axis=0, keepdims=True) + SUBLN_EPS)) * g_ref[...]
            o_ref[pl.ds(pl.multiple_of(i * TQ, TQ), TQ), hd * V_DIM:(hd + 1) * V_DIM] = (
                (y * (1.0 - lam_init)).T.astype(BF16))
        return carry

    lax.fori_loop(0, n_tiles, tile, 0)


def _attn(qs, k, vt, bias, lq1, lk1, lq2, lk2, subln_g, lam_init, batch, layer):
    t = k.shape[0]
    nq = SEQ // TQ
    hps = HEADS_PER_STEP
    per_head_scratch = ([pltpu.VMEM((TQ, 2 * TQ), F32)] * 3 + [pltpu.VMEM((1, 2 * TQ), F32)] * 4
                        + [pltpu.VMEM((VT_ROWS, 2 * TQ), F32)])
    return pl.pallas_call(
        functools.partial(_attn_kernel, lam_init=lam_init),
        grid=(ATT_HEADS // hps, batch),
        in_specs=[_layer_spec((1, QK_DIM), layer, 2)] * 4 + [
            _layer_spec((V_DIM, 1), layer, 2),
            pl.BlockSpec((nq, hps * V_DIM, 2 * TQ), lambda h, b: (b, h, 0)),
            pl.BlockSpec((SEQ, hps * V_DIM), lambda h, b: (b, h)),
            pl.BlockSpec((nq, hps * VT_ROWS, TQ), lambda h, b: (b, h, 0)),
            pl.BlockSpec((hps, TQ, 2 * TQ), lambda h, b: (h, 0, 0), pipeline_mode=pl.Buffered(1)),
            pl.BlockSpec((hps, CORNER, CORNER), lambda h, b: (h, 0, 0), pipeline_mode=pl.Buffered(1))],
        out_specs=pl.BlockSpec((SEQ, hps * V_DIM), lambda h, b: (b, h)),
        out_shape=jax.ShapeDtypeStruct((t, ATT_WIDTH), BF16),
        scratch_shapes=per_head_scratch * hps,
        compiler_params=pltpu.CompilerParams(dimension_semantics=("arbitrary",) * 2,
                                             vmem_limit_bytes=VMEM_LIMIT),
        name="attn",
    )(lq1, lk1, lq2, lk2, subln_g, qs, k, vt, *bias)


def _out_ffn_kernel(x_ref, att_ref, pc_ref, wo_ref, gffn_ref, wg_ref, wu_ref, wd_ref,
                    gfin_ref, o_ref, *, final):
    y = (jnp.dot(att_ref[...], wo_ref[:ATT_WIDTH], preferred_element_type=F32)
         + jnp.dot(pc_ref[...], wo_ref[ATT_WIDTH:], preferred_element_type=F32))
    x1 = x_ref[...] + y
    h = _rms(x1, gffn_ref[...], EPS).astype(BF16)
    gate = jnp.dot(h, wg_ref[...], preferred_element_type=F32)
    up = jnp.dot(h, wu_ref[...], preferred_element_type=F32)
    a = (jax.nn.silu(gate) * up).astype(BF16)
    x2 = x1 + jnp.dot(a, wd_ref[...], preferred_element_type=F32)
    o_ref[...] = _rms(x2, gfin_ref[...], EPS) if final else x2


def _out_ffn(x2d, att, pc, wo, gffn, wg, wu, wd, gfin, layer, final):
    t = x2d.shape[0]
    tm = TM_FFN
    const = lambda shape: _layer_spec(shape, layer, 1)
    row = lambda width: pl.BlockSpec((tm, width), lambda i: (i, 0))
    return pl.pallas_call(
        functools.partial(_out_ffn_kernel, final=final),
        grid=(t // tm,),
        in_specs=[row(D_MODEL), row(ATT_WIDTH), row(ATT_WIDTH),
                  const((D_MODEL, D_MODEL)), const((1, D_MODEL)),
                  const((D_MODEL, D_FF)), const((D_MODEL, D_FF)), const((D_FF, D_MODEL)),
                  pl.BlockSpec((1, D_MODEL), lambda i: (0, 0))],
        out_specs=row(D_MODEL),
        out_shape=jax.ShapeDtypeStruct((t, D_MODEL), F32),
        compiler_params=pltpu.CompilerParams(dimension_semantics=("arbitrary",),
                                             vmem_limit_bytes=VMEM_LIMIT),
        name="out_ffn",
    )(x2d, att, pc, wo, gffn, wg, wu, wd, gfin)


def _block_diag(w_pool):
    depth, groups = w_pool.shape[:2]
    out = jnp.zeros((depth, POOL_WIDTH, POOL_WIDTH), w_pool.dtype)
    for g in range(groups):
        out = out.at[:, g * POOL_GDIM:(g + 1) * POOL_GDIM, g * POOL_GDIM:(g + 1) * POOL_GDIM].set(w_pool[:, g])
    return out


def kernel(x, g_mix, w_in, lambda_q1, lambda_k1, lambda_q2, lambda_k2, subln_g, rel_bias,
           w_pool, pool_scale, conv_w, w_o, g_ffn, w_gate, w_up, w_down, g_final):
    batch, seq, d = x.shape
    assert (seq, d) == (SEQ, D_MODEL)
    depth = w_in.shape[0]
    xf = x.reshape(batch * seq, d)
    bias = _bias_tiles(rel_bias.astype(F32))
    w_in_b, w_o_b, w_gate_b, w_up_b, w_down_b = (w.astype(BF16) for w in (w_in, w_o, w_gate, w_up, w_down))
    wbd = _block_diag(w_pool).astype(BF16)
    as_rows = lambda v: v[:, None, :]
    for l in range(depth):
        lam_init = 0.8 - 0.6 * math.exp(-0.3 * l)
        qs, k, vt, pc = _mix_in(xf, as_rows(g_mix), w_in_b, wbd, as_rows(pool_scale), conv_w, l)
        att = _attn(qs, k, vt, bias, as_rows(lambda_q1), as_rows(lambda_k1), as_rows(lambda_q2),
                    as_rows(lambda_k2), subln_g[:, :, None], lam_init, batch, l)
        xf = _out_ffn(xf, att, pc, w_o_b, as_rows(g_ffn), w_gate_b, w_up_b, w_down_b,
                      g_final[None], l, final=(l == depth - 1))
    return xf.reshape(batch, seq, d)
```

```python
import functools
import math

import numpy as np
import jax
import jax.numpy as jnp
from jax import lax
from jax.experimental import pallas as pl
from jax.experimental.pallas import tpu as pltpu

F32 = jnp.float32
BF16 = jnp.bfloat16

D_MODEL = 1024
SEQ = 4096
ATT_HEADS = 4
QK_DIM = 64
V_DIM = 2 * QK_DIM
ATT_WIDTH = ATT_HEADS * V_DIM
BF16_ROWS = 16
VT_ROWS = V_DIM + BF16_ROWS
POOL_WINDOWS = (2, 4, 8, 16)
POOL_WIDTH = 256
POOL_GDIM = 64
CONV_WIDTH = 256
CONV_K = 3
IN_COLS = 3 * ATT_WIDTH + POOL_WIDTH + 3 * CONV_WIDTH
D_FF = 2816
NUM_BUCKETS = 32
MAX_EXACT = 16
MAX_DISTANCE = 128
EPS = 1e-6
SUBLN_EPS = 1e-5
NEG = -1e30
LOG2E = math.log2(math.e)

HALO = 16
TQ = 512
HALF = TQ // 2
CORNER = 128
PLAIN, BELOW_DIAGONAL, DIAGONAL = range(3)
HEADS_PER_STEP = 2
UNROLL = 4
TM_IN = TQ
TM_FFN = 512
VMEM_LIMIT = 56 * 1024 * 1024


def _bucket_thresholds():
    n = np.arange(MAX_DISTANCE)
    nf = np.maximum(n, 1).astype(np.float32)
    large = MAX_EXACT + (np.log(nf / MAX_EXACT) / math.log(MAX_DISTANCE / MAX_EXACT)
                         * (NUM_BUCKETS - MAX_EXACT)).astype(np.int32)
    bucket = np.where(n < MAX_EXACT, n, np.minimum(large, NUM_BUCKETS - 1))
    assert bucket[-1] == NUM_BUCKETS - 1 and np.all(np.diff(bucket) >= 0)
    return [int(np.argmax(bucket >= b)) for b in range(NUM_BUCKETS)]


_BUCKET_THR = _bucket_thresholds()


def _rms(x, g, eps):
    return (x * lax.rsqrt(jnp.mean(x * x, axis=-1, keepdims=True) + eps)) * g


def _bias_value(relb_ref, h, dist):
    val = jnp.full(dist.shape, relb_ref[0, h], F32)
    for b in range(1, NUM_BUCKETS):
        val = jnp.where(dist >= _BUCKET_THR[b], relb_ref[b, h], val)
    return jnp.where(dist >= 0, (val - relb_ref[NUM_BUCKETS - 1, h]) * LOG2E, NEG)


def _bias_kernel(relb_ref, diag_ref, corner_ref):
    h = pl.program_id(0)
    kpos = lax.broadcasted_iota(jnp.int32, (TQ, TQ), 0)
    qpos = lax.broadcasted_iota(jnp.int32, (TQ, TQ), 1)
    val = _bias_value(relb_ref, h, qpos - kpos)
    for chunk in range(4):
        diag_ref[:, chunk * HALF:(chunk + 1) * HALF] = val[:, (chunk // 2) * HALF:(chunk // 2 + 1) * HALF]
    kpos = lax.broadcasted_iota(jnp.int32, (CORNER, CORNER), 0) + (TQ - CORNER)
    qpos = lax.broadcasted_iota(jnp.int32, (CORNER, CORNER), 1)
    corner_ref[...] = _bias_value(relb_ref, h, TQ + qpos - kpos)


def _bias_tiles(rel_bias):
    assert CORNER >= MAX_DISTANCE and _BUCKET_THR[NUM_BUCKETS - 1] <= MAX_DISTANCE
    return pl.pallas_call(
        _bias_kernel,
        grid=(ATT_HEADS,),
        in_specs=[pl.BlockSpec(memory_space=pltpu.SMEM)],
        out_specs=[pl.BlockSpec((None, TQ, 2 * TQ), lambda h: (h, 0, 0)),
                   pl.BlockSpec((None, CORNER, CORNER), lambda h: (h, 0, 0))],
        out_shape=[jax.ShapeDtypeStruct((ATT_HEADS, TQ, 2 * TQ), F32),
                   jax.ShapeDtypeStruct((ATT_HEADS, CORNER, CORNER), F32)],
        name="bias_tiles",
    )(rel_bias)


def _mix_in_kernel(x_ref, g_ref, w_ref, wbd_ref, pscale_ref, convw_ref,
                   qs_ref, k_ref, vt_ref, pc_ref, carry_ref):
    tm = x_ref.shape[0]
    tiles_per_seq = SEQ // tm
    seq_tile = pl.program_id(0) % tiles_per_seq

    @pl.when(seq_tile == 0)
    def _():
        carry_ref[...] = jnp.zeros_like(carry_ref)

    h = _rms(x_ref[...], g_ref[...], EPS).astype(BF16)
    o = 3 * ATT_WIDTH
    proj = jnp.dot(h, w_ref[:, o:], preferred_element_type=F32)
    p = proj[:, :POOL_WIDTH]
    gb = proj[:, POOL_WIDTH:POOL_WIDTH + CONV_WIDTH]
    gc = proj[:, POOL_WIDTH + CONV_WIDTH:POOL_WIDTH + 2 * CONV_WIDTH]
    hin = proj[:, POOL_WIDTH + 2 * CONV_WIDTH:]
    u = gc * hin

    qkv = jnp.dot(h, w_ref[:, :o], preferred_element_type=F32)
    qt = (qkv[:, 0:ATT_WIDTH] * (QK_DIM ** -0.5 * LOG2E)).T.astype(BF16)
    first_map = lax.broadcasted_iota(jnp.int32, qt.shape, 0) % V_DIM < QK_DIM
    zero = jnp.zeros_like(qt)
    maps = (jnp.where(first_map, qt, zero), jnp.where(first_map, zero, qt))
    for chunk in range(4):
        qs_ref[:, chunk * HALF:(chunk + 1) * HALF] = maps[chunk % 2][:, (chunk // 2) * HALF:(chunk // 2 + 1) * HALF]
    k_ref[...] = qkv[:, ATT_WIDTH:2 * ATT_WIDTH].astype(BF16)
    vt = qkv[:, 2 * ATT_WIDTH:].T.astype(BF16)
    ones = jnp.ones((BF16_ROWS, tm), BF16)
    for hd in range(ATT_HEADS):
        vt_ref[hd * VT_ROWS:hd * VT_ROWS + V_DIM] = vt[hd * V_DIM:(hd + 1) * V_DIM]
        vt_ref[hd * VT_ROWS + V_DIM:(hd + 1) * VT_ROWS] = ones

    p_ext = jnp.concatenate([carry_ref[:, :POOL_WIDTH], p], axis=0)
    u_ext = jnp.concatenate([carry_ref[:, POOL_WIDTH:], u], axis=0)
    carry_ref[:, :POOL_WIDTH] = p[tm - HALO:]
    carry_ref[:, POOL_WIDTH:] = u[tm - HALO:]

    a2 = p_ext + pltpu.roll(p_ext, 1, axis=0)
    a4 = a2 + pltpu.roll(a2, 2, axis=0)
    a4_hi = a4[:, 128:]
    a8 = a4_hi + pltpu.roll(a4_hi, 4, axis=0)
    a16 = a8 + pltpu.roll(a8, 8, axis=0)
    lane = lax.broadcasted_iota(jnp.int32, (tm, 128), 1)
    low = lane < POOL_GDIM
    sums = jnp.concatenate([jnp.where(low, a2[HALO:, :128], a4[HALO:, :128]),
                            jnp.where(low, a8[HALO:], a16[HALO:])], axis=1)
    col = lax.broadcasted_iota(jnp.int32, (tm, POOL_WIDTH), 1)
    win = jnp.left_shift(2, col // POOL_GDIM)
    t1 = seq_tile * tm + lax.broadcasted_iota(jnp.int32, (tm, POOL_WIDTH), 0) + 1
    pooled = sums / jnp.minimum(t1, win).astype(F32) - p
    mixed = jnp.dot(pooled.astype(BF16), wbd_ref[...], preferred_element_type=F32) * pscale_ref[...]
    pc_ref[:, :POOL_WIDTH] = mixed.astype(BF16)

    y = (convw_ref[0:1, :] * pltpu.roll(u_ext, 2, axis=0)[HALO:]
         + convw_ref[1:2, :] * pltpu.roll(u_ext, 1, axis=0)[HALO:]
         + convw_ref[2:3, :] * u)
    pc_ref[:, POOL_WIDTH:] = (gb * y).astype(BF16)


def _layer_spec(shape, layer, grid_rank):
    index_map = {1: lambda i: (layer, 0, 0), 2: lambda h, b: (layer, 0, 0)}[grid_rank]
    return pl.BlockSpec((None,) + shape, index_map, pipeline_mode=pl.Buffered(1))


def _mix_in(x2d, g, w_in, wbd, pscale, convw, layer):
    t = x2d.shape[0]
    tm = TM_IN
    const = lambda shape: _layer_spec(shape, layer, 1)
    row = lambda width: pl.BlockSpec((tm, width), lambda i: (i, 0))
    return pl.pallas_call(
        _mix_in_kernel,
        grid=(t // tm,),
        in_specs=[row(D_MODEL), const((1, D_MODEL)), const((D_MODEL, IN_COLS)),
                  const((POOL_WIDTH, POOL_WIDTH)), const((1, POOL_WIDTH)), const((CONV_K, CONV_WIDTH))],
        out_specs=[pl.BlockSpec((None, ATT_WIDTH, 2 * tm), lambda i: (i, 0, 0)), row(ATT_WIDTH),
                   pl.BlockSpec((None, ATT_HEADS * VT_ROWS, tm), lambda i: (i, 0, 0)), row(ATT_WIDTH)],
        out_shape=[jax.ShapeDtypeStruct((t // tm, ATT_WIDTH, 2 * tm), BF16),
                   jax.ShapeDtypeStruct((t, ATT_WIDTH), BF16),
                   jax.ShapeDtypeStruct((t // tm, ATT_HEADS * VT_ROWS, tm), BF16),
                   jax.ShapeDtypeStruct((t, ATT_WIDTH), BF16)],
        scratch_shapes=[pltpu.VMEM((HALO, POOL_WIDTH + CONV_WIDTH), F32)],
        compiler_params=pltpu.CompilerParams(dimension_semantics=("arbitrary",),
                                             vmem_limit_bytes=VMEM_LIMIT),
        name="mix_in",
    )(x2d, g, w_in, wbd, pscale, convw)


def _attn_kernel(lq1_ref, lk1_ref, lq2_ref, lk2_ref, g_ref, qs_ref, k_ref, vt_ref, bias_ref, corner_ref, o_ref,
                 *scratch, lam_init):
    n_tiles = qs_ref.shape[0]
    heads = range(HEADS_PER_STEP)
    per_head = [scratch[hd * 8:(hd + 1) * 8] for hd in heads]
    buf_a = [(sc[0], sc[3]) for sc in per_head]
    buf_b = [(sc[1], sc[4]) for sc in per_head]
    buf_c = [(sc[2], sc[5]) for sc in per_head]
    m_refs = [sc[6] for sc in per_head]
    acc_refs = [sc[7] for sc in per_head]
    lam = (jnp.exp(jnp.sum(lq1_ref[...] * lk1_ref[...], axis=-1, keepdims=True))
           - jnp.exp(jnp.sum(lq2_ref[...] * lk2_ref[...], axis=-1, keepdims=True)) + lam_init)

    def write_out(tile_index):
        for hd in heads:
            out = acc_refs[hd][:V_DIM] / acc_refs[hd][V_DIM:V_DIM + 1]
            att = jnp.concatenate([out[:, 0:HALF] - lam * out[:, HALF:TQ],
                                   out[:, TQ:TQ + HALF] - lam * out[:, TQ + HALF:]], axis=1)
            y = (att * lax.rsqrt(jnp.mean(att * att, axis=0, keepdims=True) + SUBLN_EPS)) * g_ref[...]
            o_ref[pl.ds(pl.multiple_of(tile_index * TQ, TQ), TQ), hd * V_DIM:(hd + 1) * V_DIM] = (
                (y * (1.0 - lam_init)).T.astype(BF16))

    def tile(i, carry):
        def reset():
            for hd in heads:
                m_refs[hd][...] = jnp.full_like(m_refs[hd], NEG)
                acc_refs[hd][...] = jnp.zeros_like(acc_refs[hd])

        def key_rows(kind, c):
            return HALF if (kind == DIAGONAL and c == 0) else TQ

        def scores_chunk(hd, c, j, bufs, kind, qtile):
            s_ref, mx_ref = bufs[hd]
            cols = slice(hd * V_DIM, (hd + 1) * V_DIM)
            cc = slice(c * TQ, (c + 1) * TQ)
            nk = key_rows(kind, c)
            kb = k_ref[pl.ds(pl.multiple_of(j * TQ, TQ), nk), cols]
            s = jnp.dot(kb, qs_ref[qtile, cols, cc], preferred_element_type=F32)
            if kind == DIAGONAL:
                s = s + bias_ref[hd, :nk, cc]
            if kind == BELOW_DIAGONAL and c == 0:
                top, bottom = s[:TQ - CORNER], s[TQ - CORNER:]
                corner = corner_ref[hd]
                bottom = jnp.concatenate(
                    [bottom[:, :CORNER] + corner, bottom[:, CORNER:HALF],
                     bottom[:, HALF:HALF + CORNER] + corner, bottom[:, HALF + CORNER:]], axis=1)
                s_ref[:TQ - CORNER, cc] = top
                s_ref[TQ - CORNER:, cc] = bottom
                mx_ref[:, cc] = jnp.maximum(jnp.max(top, axis=0, keepdims=True),
                                            jnp.max(bottom, axis=0, keepdims=True))
                return
            s_ref[:nk, cc] = s
            mx_ref[:, cc] = jnp.max(s, axis=0, keepdims=True)

        def absorb_chunk(hd, c, j, bufs, kind):
            s_ref, mx_ref = bufs[hd]
            cc = slice(c * TQ, (c + 1) * TQ)
            nk = key_rows(kind, c)
            m_old = m_refs[hd][:, cc]
            m_new = jnp.maximum(m_old, mx_ref[:, cc])
            p = jnp.exp2(s_ref[:nk, cc] - m_new).astype(BF16)
            alpha = jnp.exp2(m_old - m_new)
            acc_refs[hd][:, cc] = alpha * acc_refs[hd][:, cc] + jnp.dot(
                vt_ref[j, hd * VT_ROWS:(hd + 1) * VT_ROWS, :nk], p, preferred_element_type=F32)
            m_refs[hd][:, cc] = m_new

        def fused(score=None, absorb=None):
            for hd in heads:
                for c in range(2):
                    if score is not None:
                        scores_chunk(hd, c, *score)
                    if absorb is not None:
                        absorb_chunk(hd, c, *absorb)

        def run(j, nsteps, pending, free):
            for step in range(nsteps):
                fused(score=(j - step - 1, free, PLAIN, i), absorb=(j - step, pending, PLAIN))
                pending, free = free, pending
            return pending

        def finish(pending, kind=PLAIN):
            nxt = jnp.minimum(i + 1, n_tiles - 1)
            fused(score=(nxt, buf_c, DIAGONAL, nxt), absorb=(0, pending, kind))

        @pl.when(i == 0)
        def _():
            reset()
            fused(score=(0, buf_a, DIAGONAL, i))
            finish(buf_a, DIAGONAL)

        @pl.when(i >= 1)
        def _():
            write_out(i - 1)
            reset()
            fused(score=(i - 1, buf_b, BELOW_DIAGONAL, i), absorb=(i, buf_c, DIAGONAL))
            n_plain = i - 1

            def unrolled(t, c):
                run(i - 1 - UNROLL * t, UNROLL, buf_b, buf_a)
                return c

            lax.fori_loop(0, n_plain // UNROLL, unrolled, 0)
            rest = n_plain % UNROLL
            for r in range(UNROLL):
                @pl.when(rest == r)
                def _():
                    finish(run(r, r, buf_b, buf_a))

        return carry

    lax.fori_loop(0, n_tiles, tile, 0)
    write_out(n_tiles - 1)


def _attn(qs, k, vt, bias, lq1, lk1, lq2, lk2, subln_g, lam_init, batch, layer):
    t = k.shape[0]
    nq = SEQ // TQ
    hps = HEADS_PER_STEP
    per_head_scratch = ([pltpu.VMEM((TQ, 2 * TQ), F32)] * 3 + [pltpu.VMEM((1, 2 * TQ), F32)] * 4
                        + [pltpu.VMEM((VT_ROWS, 2 * TQ), F32)])
    return pl.pallas_call(
        functools.partial(_attn_kernel, lam_init=lam_init),
        grid=(ATT_HEADS // hps, batch),
        in_specs=[_layer_spec((1, QK_DIM), layer, 2)] * 4 + [
            _layer_spec((V_DIM, 1), layer, 2),
            pl.BlockSpec((nq, hps * V_DIM, 2 * TQ), lambda h, b: (b, h, 0)),
            pl.BlockSpec((SEQ, hps * V_DIM), lambda h, b: (b, h)),
            pl.BlockSpec((nq, hps * VT_ROWS, TQ), lambda h, b: (b, h, 0)),
            pl.BlockSpec((hps, TQ, 2 * TQ), lambda h, b: (h, 0, 0), pipeline_mode=pl.Buffered(1)),
            pl.BlockSpec((hps, CORNER, CORNER), lambda h, b: (h, 0, 0), pipeline_mode=pl.Buffered(1))],
        out_specs=pl.BlockSpec((SEQ, hps * V_DIM), lambda h, b: (b, h)),
        out_shape=jax.ShapeDtypeStruct((t, ATT_WIDTH), BF16),
        scratch_shapes=per_head_scratch * hps,
        compiler_params=pltpu.CompilerParams(dimension_semantics=("arbitrary",) * 2,
                                             vmem_limit_bytes=VMEM_LIMIT),
        name="attn",
    )(lq1, lk1, lq2, lk2, subln_g, qs, k, vt, *bias)


def _out_ffn_kernel(x_ref, att_ref, pc_ref, wo_ref, gffn_ref, wg_ref, wu_ref, wd_ref,
                    gfin_ref, o_ref, *, final):
    y = (jnp.dot(att_ref[...], wo_ref[:ATT_WIDTH], preferred_element_type=F32)
         + jnp.dot(pc_ref[...], wo_ref[ATT_WIDTH:], preferred_element_type=F32))
    x1 = x_ref[...] + y
    h = _rms(x1, gffn_ref[...], EPS).astype(BF16)
    gate = jnp.dot(h, wg_ref[...], preferred_element_type=F32)
    up = jnp.dot(h, wu_ref[...], preferred_element_type=F32)
    a = (jax.nn.silu(gate) * up).astype(BF16)
    x2 = x1 + jnp.dot(a, wd_ref[...], preferred_element_type=F32)
    o_ref[...] = _rms(x2, gfin_ref[...], EPS) if final else x2


def _out_ffn(x2d, att, pc, wo, gffn, wg, wu, wd, gfin, layer, final):
    t = x2d.shape[0]
    tm = TM_FFN
    const = lambda shape: _layer_spec(shape, layer, 1)
    row = lambda width: pl.BlockSpec((tm, width), lambda i: (i, 0))
    return pl.pallas_call(
        functools.partial(_out_ffn_kernel, final=final),
        grid=(t // tm,),
        in_specs=[row(D_MODEL), row(ATT_WIDTH), row(ATT_WIDTH),
                  const((D_MODEL, D_MODEL)), const((1, D_MODEL)),
                  const((D_MODEL, D_FF)), const((D_MODEL, D_FF)), const((D_FF, D_MODEL)),
                  pl.BlockSpec((1, D_MODEL), lambda i: (0, 0))],
        out_specs=row(D_MODEL),
        out_shape=jax.ShapeDtypeStruct((t, D_MODEL), F32),
        compiler_params=pltpu.CompilerParams(dimension_semantics=("arbitrary",),
                                             vmem_limit_bytes=VMEM_LIMIT),
        name="out_ffn",
    )(x2d, att, pc, wo, gffn, wg, wu, wd, gfin)


def _block_diag(w_pool):
    depth, groups = w_pool.shape[:2]
    out = jnp.zeros((depth, POOL_WIDTH, POOL_WIDTH), w_pool.dtype)
    for g in range(groups):
        out = out.at[:, g * POOL_GDIM:(g + 1) * POOL_GDIM, g * POOL_GDIM:(g + 1) * POOL_GDIM].set(w_pool[:, g])
    return out


def kernel(x, g_mix, w_in, lambda_q1, lambda_k1, lambda_q2, lambda_k2, subln_g, rel_bias,
           w_pool, pool_scale, conv_w, w_o, g_ffn, w_gate, w_up, w_down, g_final):
    batch, seq, d = x.shape
    assert (seq, d) == (SEQ, D_MODEL)
    depth = w_in.shape[0]
    xf = x.reshape(batch * seq, d)
    bias = _bias_tiles(rel_bias.astype(F32))
    w_in_b, w_o_b, w_gate_b, w_up_b, w_down_b = (w.astype(BF16) for w in (w_in, w_o, w_gate, w_up, w_down))
    wbd = _block_diag(w_pool).astype(BF16)
    as_rows = lambda v: v[:, None, :]
    for l in range(depth):
        lam_init = 0.8 - 0.6 * math.exp(-0.3 * l)
        qs, k, vt, pc = _mix_in(xf, as_rows(g_mix), w_in_b, wbd, as_rows(pool_scale), conv_w, l)
        att = _attn(qs, k, vt, bias, as_rows(lambda_q1), as_rows(lambda_k1), as_rows(lambda_q2),
                    as_rows(lambda_k2), subln_g[:, :, None], lam_init, batch, l)
        xf = _out_ffn(xf, att, pc, w_o_b, as_rows(g_ffn), w_gate_b, w_up_b, w_down_b,
                      g_final[None], l, final=(l == depth - 1))
    return xf.reshape(batch, seq, d)
```

```python
import functools
import math

import numpy as np
import jax
import jax.numpy as jnp
from jax import lax
from jax.experimental import pallas as pl
from jax.experimental.pallas import tpu as pltpu

F32 = jnp.float32
BF16 = jnp.bfloat16

D_MODEL = 1024
SEQ = 4096
ATT_HEADS = 4
QK_DIM = 64
V_DIM = 2 * QK_DIM
ATT_WIDTH = ATT_HEADS * V_DIM
BF16_ROWS = 16
VT_ROWS = V_DIM + BF16_ROWS
POOL_WINDOWS = (2, 4, 8, 16)
POOL_WIDTH = 256
POOL_GDIM = 64
CONV_WIDTH = 256
CONV_K = 3
IN_COLS = 3 * ATT_WIDTH + POOL_WIDTH + 3 * CONV_WIDTH
D_FF = 2816
NUM_BUCKETS = 32
MAX_EXACT = 16
MAX_DISTANCE = 128
EPS = 1e-6
SUBLN_EPS = 1e-5
NEG = -1e30
LOG2E = math.log2(math.e)

HALO = 16
TQ = 512
HALF = TQ // 2
CORNER = 128
PLAIN, BELOW_DIAGONAL, DIAGONAL = range(3)
HEADS_PER_STEP = 2
UNROLL = 4
TM_IN = TQ
TM_FFN = 512
VMEM_LIMIT = 56 * 1024 * 1024


def _bucket_thresholds():
    n = np.arange(MAX_DISTANCE)
    nf = np.maximum(n, 1).astype(np.float32)
    large = MAX_EXACT + (np.log(nf / MAX_EXACT) / math.log(MAX_DISTANCE / MAX_EXACT)
                         * (NUM_BUCKETS - MAX_EXACT)).astype(np.int32)
    bucket = np.where(n < MAX_EXACT, n, np.minimum(large, NUM_BUCKETS - 1))
    assert bucket[-1] == NUM_BUCKETS - 1 and np.all(np.diff(bucket) >= 0)
    return [int(np.argmax(bucket >= b)) for b in range(NUM_BUCKETS)]


_BUCKET_THR = _bucket_thresholds()


def _rms(x, g, eps):
    return (x * lax.rsqrt(jnp.mean(x * x, axis=-1, keepdims=True) + eps)) * g


def _bias_value(relb_ref, h, dist):
    val = jnp.full(dist.shape, relb_ref[0, h], F32)
    for b in range(1, NUM_BUCKETS):
        val = jnp.where(dist >= _BUCKET_THR[b], relb_ref[b, h], val)
    return jnp.where(dist >= 0, (val - relb_ref[NUM_BUCKETS - 1, h]) * LOG2E, NEG)


def _bias_kernel(relb_ref, diag_ref, corner_ref):
    h = pl.program_id(0)
    kpos = lax.broadcasted_iota(jnp.int32, (TQ, TQ), 0)
    qpos = lax.broadcasted_iota(jnp.int32, (TQ, TQ), 1)
    val = _bias_value(relb_ref, h, qpos - kpos)
    for chunk in range(4):
        diag_ref[:, chunk * HALF:(chunk + 1) * HALF] = val[:, (chunk // 2) * HALF:(chunk // 2 + 1) * HALF]
    kpos = lax.broadcasted_iota(jnp.int32, (CORNER, CORNER), 0) + (TQ - CORNER)
    qpos = lax.broadcasted_iota(jnp.int32, (CORNER, CORNER), 1)
    corner_ref[...] = _bias_value(relb_ref, h, TQ + qpos - kpos)


def _bias_tiles(rel_bias):
    assert CORNER >= MAX_DISTANCE and _BUCKET_THR[NUM_BUCKETS - 1] <= MAX_DISTANCE
    return pl.pallas_call(
        _bias_kernel,
        grid=(ATT_HEADS,),
        in_specs=[pl.BlockSpec(memory_space=pltpu.SMEM)],
        out_specs=[pl.BlockSpec((None, TQ, 2 * TQ), lambda h: (h, 0, 0)),
                   pl.BlockSpec((None, CORNER, CORNER), lambda h: (h, 0, 0))],
        out_shape=[jax.ShapeDtypeStruct((ATT_HEADS, TQ, 2 * TQ), F32),
                   jax.ShapeDtypeStruct((ATT_HEADS, CORNER, CORNER), F32)],
        name="bias_tiles",
    )(rel_bias)


def _mix_in_kernel(x_ref, g_ref, w_ref, wbd_ref, pscale_ref, convw_ref,
                   qs_ref, k_ref, vt_ref, pc_ref, carry_ref):
    tm = x_ref.shape[0]
    tiles_per_seq = SEQ // tm
    seq_tile = pl.program_id(0) % tiles_per_seq

    @pl.when(seq_tile == 0)
    def _():
        carry_ref[...] = jnp.zeros_like(carry_ref)

    o = 3 * ATT_WIDTH
    halves = [_rms(x_ref[r * (tm // 2):(r + 1) * (tm // 2), :], g_ref[...], EPS).astype(BF16) for r in range(2)]
    proj = jnp.concatenate([jnp.dot(hh, w_ref[:, o:], preferred_element_type=F32) for hh in halves], axis=0)
    h = jnp.concatenate(halves, axis=0)
    p = proj[:, :POOL_WIDTH]
    gb = proj[:, POOL_WIDTH:POOL_WIDTH + CONV_WIDTH]
    gc = proj[:, POOL_WIDTH + CONV_WIDTH:POOL_WIDTH + 2 * CONV_WIDTH]
    hin = proj[:, POOL_WIDTH + 2 * CONV_WIDTH:]
    u = gc * hin

    def project(lo, hi):
        return jnp.dot(h, w_ref[:, lo:hi], preferred_element_type=F32)

    vt = project(2 * ATT_WIDTH, 3 * ATT_WIDTH).T.astype(BF16)
    ones = jnp.ones((BF16_ROWS, tm), BF16)
    for hd in range(ATT_HEADS):
        vt_ref[hd * VT_ROWS:hd * VT_ROWS + V_DIM] = vt[hd * V_DIM:(hd + 1) * V_DIM]
        vt_ref[hd * VT_ROWS + V_DIM:(hd + 1) * VT_ROWS] = ones

    p_ext = jnp.concatenate([carry_ref[:, :POOL_WIDTH], p], axis=0)
    u_ext = jnp.concatenate([carry_ref[:, POOL_WIDTH:], u], axis=0)
    carry_ref[:, :POOL_WIDTH] = p[tm - HALO:]
    carry_ref[:, POOL_WIDTH:] = u[tm - HALO:]

    a2 = p_ext + pltpu.roll(p_ext, 1, axis=0)
    a4 = a2 + pltpu.roll(a2, 2, axis=0)
    a4_hi = a4[:, 128:]
    a8 = a4_hi + pltpu.roll(a4_hi, 4, axis=0)
    a16 = a8 + pltpu.roll(a8, 8, axis=0)
    lane = lax.broadcasted_iota(jnp.int32, (tm, 128), 1)
    low = lane < POOL_GDIM
    sums = jnp.concatenate([jnp.where(low, a2[HALO:, :128], a4[HALO:, :128]),
                            jnp.where(low, a8[HALO:], a16[HALO:])], axis=1)
    col = lax.broadcasted_iota(jnp.int32, (tm, POOL_WIDTH), 1)
    win = jnp.left_shift(2, col // POOL_GDIM)
    t1 = seq_tile * tm + lax.broadcasted_iota(jnp.int32, (tm, POOL_WIDTH), 0) + 1
    pooled = sums / jnp.minimum(t1, win).astype(F32) - p
    mixed = jnp.dot(pooled.astype(BF16), wbd_ref[...], preferred_element_type=F32) * pscale_ref[...]
    pc_ref[:, :POOL_WIDTH] = mixed.astype(BF16)

    y = (convw_ref[0:1, :] * pltpu.roll(u_ext, 2, axis=0)[HALO:]
         + convw_ref[1:2, :] * pltpu.roll(u_ext, 1, axis=0)[HALO:]
         + convw_ref[2:3, :] * u)
    pc_ref[:, POOL_WIDTH:] = (gb * y).astype(BF16)

    qt = (project(0, ATT_WIDTH) * (QK_DIM ** -0.5 * LOG2E)).T.astype(BF16)
    first_map = lax.broadcasted_iota(jnp.int32, qt.shape, 0) % V_DIM < QK_DIM
    zero = jnp.zeros_like(qt)
    maps = (jnp.where(first_map, qt, zero), jnp.where(first_map, zero, qt))
    for chunk in range(4):
        qs_ref[:, chunk * HALF:(chunk + 1) * HALF] = maps[chunk % 2][:, (chunk // 2) * HALF:(chunk // 2 + 1) * HALF]
    k_ref[...] = project(ATT_WIDTH, 2 * ATT_WIDTH).astype(BF16)


def _layer_spec(shape, layer, grid_rank):
    index_map = {1: lambda i: (layer, 0, 0), 2: lambda h, b: (layer, 0, 0)}[grid_rank]
    return pl.BlockSpec((None,) + shape, index_map, pipeline_mode=pl.Buffered(1))


def _mix_in(x2d, g, w_in, wbd, pscale, convw, layer):
    t = x2d.shape[0]
    tm = TM_IN
    const = lambda shape: _layer_spec(shape, layer, 1)
    row = lambda width: pl.BlockSpec((tm, width), lambda i: (i, 0))
    return pl.pallas_call(
        _mix_in_kernel,
        grid=(t // tm,),
        in_specs=[row(D_MODEL), const((1, D_MODEL)), const((D_MODEL, IN_COLS)),
                  const((POOL_WIDTH, POOL_WIDTH)), const((1, POOL_WIDTH)), const((CONV_K, CONV_WIDTH))],
        out_specs=[pl.BlockSpec((None, ATT_WIDTH, 2 * tm), lambda i: (i, 0, 0)), row(ATT_WIDTH),
                   pl.BlockSpec((None, ATT_HEADS * VT_ROWS, tm), lambda i: (i, 0, 0)), row(ATT_WIDTH)],
        out_shape=[jax.ShapeDtypeStruct((t // tm, ATT_WIDTH, 2 * tm), BF16),
                   jax.ShapeDtypeStruct((t, ATT_WIDTH), BF16),
                   jax.ShapeDtypeStruct((t // tm, ATT_HEADS * VT_ROWS, tm), BF16),
                   jax.ShapeDtypeStruct((t, ATT_WIDTH), BF16)],
        scratch_shapes=[pltpu.VMEM((HALO, POOL_WIDTH + CONV_WIDTH), F32)],
        compiler_params=pltpu.CompilerParams(dimension_semantics=("arbitrary",),
                                             vmem_limit_bytes=VMEM_LIMIT),
        name="mix_in",
    )(x2d, g, w_in, wbd, pscale, convw)


def _attn_kernel(lq1_ref, lk1_ref, lq2_ref, lk2_ref, g_ref, qs_ref, k_ref, vt_ref, bias_ref, corner_ref, o_ref,
                 *scratch, lam_init):
    n_tiles = qs_ref.shape[0]
    heads = range(HEADS_PER_STEP)
    per_head = [scratch[hd * 8:(hd + 1) * 8] for hd in heads]
    buf_a = [(sc[0], sc[3]) for sc in per_head]
    buf_b = [(sc[1], sc[4]) for sc in per_head]
    buf_c = [(sc[2], sc[5]) for sc in per_head]
    m_refs = [sc[6] for sc in per_head]
    acc_refs = [sc[7] for sc in per_head]
    lam = (jnp.exp(jnp.sum(lq1_ref[...] * lk1_ref[...], axis=-1, keepdims=True))
           - jnp.exp(jnp.sum(lq2_ref[...] * lk2_ref[...], axis=-1, keepdims=True)) + lam_init)

    def write_out(tile_index):
        for hd in heads:
            out = acc_refs[hd][:V_DIM] / acc_refs[hd][V_DIM:V_DIM + 1]
            att = jnp.concatenate([out[:, 0:HALF] - lam * out[:, HALF:TQ],
                                   out[:, TQ:TQ + HALF] - lam * out[:, TQ + HALF:]], axis=1)
            y = (att * lax.rsqrt(jnp.mean(att * att, axis=0, keepdims=True) + SUBLN_EPS)) * g_ref[...]
            o_ref[pl.ds(pl.multiple_of(tile_index * TQ, TQ), TQ), hd * V_DIM:(hd + 1) * V_DIM] = (
                (y * (1.0 - lam_init)).T.astype(BF16))

    def tile(i, carry):
        def reset():
            for hd in heads:
                m_refs[hd][...] = jnp.full_like(m_refs[hd], NEG)
                acc_refs[hd][...] = jnp.zeros_like(acc_refs[hd])

        def key_rows(kind, c):
            return HALF if (kind == DIAGONAL and c == 0) else TQ

        def scores_chunk(hd, c, j, bufs, kind, qtile):
            s_ref, mx_ref = bufs[hd]
            cols = slice(hd * V_DIM, (hd + 1) * V_DIM)
            cc = slice(c * TQ, (c + 1) * TQ)
            nk = key_rows(kind, c)
            kb = k_ref[pl.ds(pl.multiple_of(j * TQ, TQ), nk), cols]
            s = jnp.dot(kb, qs_ref[qtile, cols, cc], preferred_element_type=F32)
            if kind == DIAGONAL:
                s = s + bias_ref[hd, :nk, cc]
            if kind == BELOW_DIAGONAL and c == 0:
                top, bottom = s[:TQ - CORNER], s[TQ - CORNER:]
                corner = corner_ref[hd]
                bottom = jnp.concatenate(
                    [bottom[:, :CORNER] + corner, bottom[:, CORNER:HALF],
                     bottom[:, HALF:HALF + CORNER] + corner, bottom[:, HALF + CORNER:]], axis=1)
                s_ref[:TQ - CORNER, cc] = top
                s_ref[TQ - CORNER:, cc] = bottom
                mx_ref[:, cc] = jnp.maximum(jnp.max(top, axis=0, keepdims=True),
                                            jnp.max(bottom, axis=0, keepdims=True))
                return
            s_ref[:nk, cc] = s
            mx_ref[:, cc] = jnp.max(s, axis=0, keepdims=True)

        def absorb_chunk(hd, c, j, bufs, kind):
            s_ref, mx_ref = bufs[hd]
            cc = slice(c * TQ, (c + 1) * TQ)
            nk = key_rows(kind, c)
            m_old = m_refs[hd][:, cc]
            m_new = jnp.maximum(m_old, mx_ref[:, cc])
            p = jnp.exp2(s_ref[:nk, cc] - m_new).astype(BF16)
            alpha = jnp.exp2(m_old - m_new)
            acc_refs[hd][:, cc] = alpha * acc_refs[hd][:, cc] + jnp.dot(
                vt_ref[j, hd * VT_ROWS:(hd + 1) * VT_ROWS, :nk], p, preferred_element_type=F32)
            m_refs[hd][:, cc] = m_new

        def fused(score=None, absorb=None):
            for hd in heads:
                for c in range(2):
                    if score is not None:
                        scores_chunk(hd, c, *score)
                    if absorb is not None:
                        absorb_chunk(hd, c, *absorb)

        def run(j, nsteps, pending, free):
            for step in range(nsteps):
                fused(score=(j - step - 1, free, PLAIN, i), absorb=(j - step, pending, PLAIN))
                pending, free = free, pending
            return pending

        def finish(pending, kind=PLAIN):
            nxt = jnp.minimum(i + 1, n_tiles - 1)
            fused(score=(nxt, buf_c, DIAGONAL, nxt), absorb=(0, pending, kind))

        @pl.when(i == 0)
        def _():
            reset()
            fused(score=(0, buf_a, DIAGONAL, i))
            finish(buf_a, DIAGONAL)

        @pl.when(i >= 1)
        def _():
            write_out(i - 1)
            reset()
            fused(score=(i - 1, buf_b, BELOW_DIAGONAL, i), absorb=(i, buf_c, DIAGONAL))
            n_plain = i - 1

            def unrolled(t, c):
                run(i - 1 - UNROLL * t, UNROLL, buf_b, buf_a)
                return c

            lax.fori_loop(0, n_plain // UNROLL, unrolled, 0)
            rest = n_plain % UNROLL
            for r in range(UNROLL):
                @pl.when(rest == r)
                def _():
                    finish(run(r, r, buf_b, buf_a))

        return carry

    lax.fori_loop(0, n_tiles, tile, 0)
    write_out(n_tiles - 1)


def _attn(qs, k, vt, bias, lq1, lk1, lq2, lk2, subln_g, lam_init, batch, layer):
    t = k.shape[0]
    nq = SEQ // TQ
    hps = HEADS_PER_STEP
    per_head_scratch = ([pltpu.VMEM((TQ, 2 * TQ), F32)] * 3 + [pltpu.VMEM((1, 2 * TQ), F32)] * 4
                        + [pltpu.VMEM((VT_ROWS, 2 * TQ), F32)])
    return pl.pallas_call(
        functools.partial(_attn_kernel, lam_init=lam_init),
        grid=(ATT_HEADS // hps, batch),
        in_specs=[_layer_spec((1, QK_DIM), layer, 2)] * 4 + [
            _layer_spec((V_DIM, 1), layer, 2),
            pl.BlockSpec((nq, hps * V_DIM, 2 * TQ), lambda h, b: (b, h, 0)),
            pl.BlockSpec((SEQ, hps * V_DIM), lambda h, b: (b, h)),
            pl.BlockSpec((nq, hps * VT_ROWS, TQ), lambda h, b: (b, h, 0)),
            pl.BlockSpec((hps, TQ, 2 * TQ), lambda h, b: (h, 0, 0), pipeline_mode=pl.Buffered(1)),
            pl.BlockSpec((hps, CORNER, CORNER), lambda h, b: (h, 0, 0), pipeline_mode=pl.Buffered(1))],
        out_specs=pl.BlockSpec((SEQ, hps * V_DIM), lambda h, b: (b, h)),
        out_shape=jax.ShapeDtypeStruct((t, ATT_WIDTH), BF16),
        scratch_shapes=per_head_scratch * hps,
        compiler_params=pltpu.CompilerParams(dimension_semantics=("arbitrary",) * 2,
                                             vmem_limit_bytes=VMEM_LIMIT),
        name="attn",
    )(lq1, lk1, lq2, lk2, subln_g, qs, k, vt, *bias)


def _out_ffn_kernel(x_ref, att_ref, pc_ref, wo_ref, gffn_ref, wg_ref, wu_ref, wd_ref,
                    gfin_ref, o_ref, *, final):
    y = (jnp.dot(att_ref[...], wo_ref[:ATT_WIDTH], preferred_element_type=F32)
         + jnp.dot(pc_ref[...], wo_ref[ATT_WIDTH:], preferred_element_type=F32))
    x1 = x_ref[...] + y
    h = _rms(x1, gffn_ref[...], EPS).astype(BF16)
    gate = jnp.dot(h, wg_ref[...], preferred_element_type=F32)
    up = jnp.dot(h, wu_ref[...], preferred_element_type=F32)
    a = (jax.nn.silu(gate) * up).astype(BF16)
    x2 = x1 + jnp.dot(a, wd_ref[...], preferred_element_type=F32)
    o_ref[...] = _rms(x2, gfin_ref[...], EPS) if final else x2


def _out_ffn(x2d, att, pc, wo, gffn, wg, wu, wd, gfin, layer, final):
    t = x2d.shape[0]
    tm = TM_FFN
    const = lambda shape: _layer_spec(shape, layer, 1)
    row = lambda width: pl.BlockSpec((tm, width), lambda i: (i, 0))
    return pl.pallas_call(
        functools.partial(_out_ffn_kernel, final=final),
        grid=(t // tm,),
        in_specs=[row(D_MODEL), row(ATT_WIDTH), row(ATT_WIDTH),
                  const((D_MODEL, D_MODEL)), const((1, D_MODEL)),
                  const((D_MODEL, D_FF)), const((D_MODEL, D_FF)), const((D_FF, D_MODEL)),
                  pl.BlockSpec((1, D_MODEL), lambda i: (0, 0))],
        out_specs=row(D_MODEL),
        out_shape=jax.ShapeDtypeStruct((t, D_MODEL), F32),
        compiler_params=pltpu.CompilerParams(dimension_semantics=("arbitrary",),
                                             vmem_limit_bytes=VMEM_LIMIT),
        name="out_ffn",
    )(x2d, att, pc, wo, gffn, wg, wu, wd, gfin)


def _block_diag(w_pool):
    depth, groups = w_pool.shape[:2]
    out = jnp.zeros((depth, POOL_WIDTH, POOL_WIDTH), w_pool.dtype)
    for g in range(groups):
        out = out.at[:, g * POOL_GDIM:(g + 1) * POOL_GDIM, g * POOL_GDIM:(g + 1) * POOL_GDIM].set(w_pool[:, g])
    return out


def kernel(x, g_mix, w_in, lambda_q1, lambda_k1, lambda_q2, lambda_k2, subln_g, rel_bias,
           w_pool, pool_scale, conv_w, w_o, g_ffn, w_gate, w_up, w_down, g_final):
    batch, seq, d = x.shape
    assert (seq, d) == (SEQ, D_MODEL)
    depth = w_in.shape[0]
    xf = x.reshape(batch * seq, d)
    bias = _bias_tiles(rel_bias.astype(F32))
    w_in_b, w_o_b, w_gate_b, w_up_b, w_down_b = (w.astype(BF16) for w in (w_in, w_o, w_gate, w_up, w_down))
    wbd = _block_diag(w_pool).astype(BF16)
    as_rows = lambda v: v[:, None, :]
    for l in range(depth):
        lam_init = 0.8 - 0.6 * math.exp(-0.3 * l)
        qs, k, vt, pc = _mix_in(xf, as_rows(g_mix), w_in_b, wbd, as_rows(pool_scale), conv_w, l)
        att = _attn(qs, k, vt, bias, as_rows(lambda_q1), as_rows(lambda_k1), as_rows(lambda_q2),
                    as_rows(lambda_k2), subln_g[:, :, None], lam_init, batch, l)
        xf = _out_ffn(xf, att, pc, w_o_b, as_rows(g_ffn), w_gate_b, w_up_b, w_down_b,
                      g_final[None], l, final=(l == depth - 1))
    return xf.reshape(batch, seq, d)
```

```python
import functools
import math

import numpy as np
import jax
import jax.numpy as jnp
from jax import lax
from jax.experimental import pallas as pl
from jax.experimental.pallas import tpu as pltpu

F32 = jnp.float32
BF16 = jnp.bfloat16

D_MODEL = 1024
SEQ = 4096
ATT_HEADS = 4
QK_DIM = 64
V_DIM = 2 * QK_DIM
ATT_WIDTH = ATT_HEADS * V_DIM
BF16_ROWS = 16
VT_ROWS = V_DIM + BF16_ROWS
POOL_WINDOWS = (2, 4, 8, 16)
POOL_WIDTH = 256
POOL_GDIM = 64
CONV_WIDTH = 256
CONV_K = 3
IN_COLS = 3 * ATT_WIDTH + POOL_WIDTH + 3 * CONV_WIDTH
D_FF = 2816
NUM_BUCKETS = 32
MAX_EXACT = 16
MAX_DISTANCE = 128
EPS = 1e-6
SUBLN_EPS = 1e-5
NEG = -1e30
LOG2E = math.log2(math.e)

HALO = 16
TQ = 512
HALF = TQ // 2
CORNER = 128
PLAIN, BELOW_DIAGONAL, DIAGONAL = range(3)
HEADS_PER_STEP = 2
N_TILES = SEQ // TQ
TM_IN = TQ
TM_FFN = 512
VMEM_LIMIT = 56 * 1024 * 1024


def _bucket_thresholds():
    n = np.arange(MAX_DISTANCE)
    nf = np.maximum(n, 1).astype(np.float32)
    large = MAX_EXACT + (np.log(nf / MAX_EXACT) / math.log(MAX_DISTANCE / MAX_EXACT)
                         * (NUM_BUCKETS - MAX_EXACT)).astype(np.int32)
    bucket = np.where(n < MAX_EXACT, n, np.minimum(large, NUM_BUCKETS - 1))
    assert bucket[-1] == NUM_BUCKETS - 1 and np.all(np.diff(bucket) >= 0)
    return [int(np.argmax(bucket >= b)) for b in range(NUM_BUCKETS)]


_BUCKET_THR = _bucket_thresholds()


def _rms(x, g, eps):
    return (x * lax.rsqrt(jnp.mean(x * x, axis=-1, keepdims=True) + eps)) * g


def _bias_value(relb_ref, h, dist):
    val = jnp.full(dist.shape, relb_ref[0, h], F32)
    for b in range(1, NUM_BUCKETS):
        val = jnp.where(dist >= _BUCKET_THR[b], relb_ref[b, h], val)
    return jnp.where(dist >= 0, (val - relb_ref[NUM_BUCKETS - 1, h]) * LOG2E, NEG)


def _bias_kernel(relb_ref, diag_ref, corner_ref):
    h = pl.program_id(0)
    kpos = lax.broadcasted_iota(jnp.int32, (TQ, TQ), 0)
    qpos = lax.broadcasted_iota(jnp.int32, (TQ, TQ), 1)
    val = _bias_value(relb_ref, h, qpos - kpos)
    for chunk in range(4):
        diag_ref[:, chunk * HALF:(chunk + 1) * HALF] = val[:, (chunk // 2) * HALF:(chunk // 2 + 1) * HALF]
    kpos = lax.broadcasted_iota(jnp.int32, (CORNER, CORNER), 0) + (TQ - CORNER)
    qpos = lax.broadcasted_iota(jnp.int32, (CORNER, CORNER), 1)
    corner_ref[...] = _bias_value(relb_ref, h, TQ + qpos - kpos)


def _bias_tiles(rel_bias):
    assert CORNER >= MAX_DISTANCE and _BUCKET_THR[NUM_BUCKETS - 1] <= MAX_DISTANCE
    return pl.pallas_call(
        _bias_kernel,
        grid=(ATT_HEADS,),
        in_specs=[pl.BlockSpec(memory_space=pltpu.SMEM)],
        out_specs=[pl.BlockSpec((None, TQ, 2 * TQ), lambda h: (h, 0, 0)),
                   pl.BlockSpec((None, CORNER, CORNER), lambda h: (h, 0, 0))],
        out_shape=[jax.ShapeDtypeStruct((ATT_HEADS, TQ, 2 * TQ), F32),
                   jax.ShapeDtypeStruct((ATT_HEADS, CORNER, CORNER), F32)],
        name="bias_tiles",
    )(rel_bias)


def _mix_in_kernel(x_ref, g_ref, w_ref, wbd_ref, pscale_ref, convw_ref,
                   qs_ref, k_ref, vt_ref, pc_ref, carry_ref):
    tm = x_ref.shape[0]
    tiles_per_seq = SEQ // tm
    seq_tile = pl.program_id(0) % tiles_per_seq

    @pl.when(seq_tile == 0)
    def _():
        carry_ref[...] = jnp.zeros_like(carry_ref)

    h = _rms(x_ref[...], g_ref[...], EPS).astype(BF16)
    o = 3 * ATT_WIDTH
    proj = jnp.dot(h, w_ref[:, o:], preferred_element_type=F32)
    p = proj[:, :POOL_WIDTH]
    gb = proj[:, POOL_WIDTH:POOL_WIDTH + CONV_WIDTH]
    gc = proj[:, POOL_WIDTH + CONV_WIDTH:POOL_WIDTH + 2 * CONV_WIDTH]
    hin = proj[:, POOL_WIDTH + 2 * CONV_WIDTH:]
    u = gc * hin

    qkv = jnp.dot(h, w_ref[:, :o], preferred_element_type=F32)
    qt = (qkv[:, 0:ATT_WIDTH] * (QK_DIM ** -0.5 * LOG2E)).T.astype(BF16)
    first_map = lax.broadcasted_iota(jnp.int32, qt.shape, 0) % V_DIM < QK_DIM
    zero = jnp.zeros_like(qt)
    maps = (jnp.where(first_map, qt, zero), jnp.where(first_map, zero, qt))
    for chunk in range(4):
        qs_ref[:, chunk * HALF:(chunk + 1) * HALF] = maps[chunk % 2][:, (chunk // 2) * HALF:(chunk // 2 + 1) * HALF]
    k_ref[...] = qkv[:, ATT_WIDTH:2 * ATT_WIDTH].astype(BF16)
    vt = qkv[:, 2 * ATT_WIDTH:].T.astype(BF16)
    ones = jnp.ones((BF16_ROWS, tm), BF16)
    for hd in range(ATT_HEADS):
        vt_ref[hd * VT_ROWS:hd * VT_ROWS + V_DIM] = vt[hd * V_DIM:(hd + 1) * V_DIM]
        vt_ref[hd * VT_ROWS + V_DIM:(hd + 1) * VT_ROWS] = ones

    p_ext = jnp.concatenate([carry_ref[:, :POOL_WIDTH], p], axis=0)
    u_ext = jnp.concatenate([carry_ref[:, POOL_WIDTH:], u], axis=0)
    carry_ref[:, :POOL_WIDTH] = p[tm - HALO:]
    carry_ref[:, POOL_WIDTH:] = u[tm - HALO:]

    a2 = p_ext + pltpu.roll(p_ext, 1, axis=0)
    a4 = a2 + pltpu.roll(a2, 2, axis=0)
    a4_hi = a4[:, 128:]
    a8 = a4_hi + pltpu.roll(a4_hi, 4, axis=0)
    a16 = a8 + pltpu.roll(a8, 8, axis=0)
    lane = lax.broadcasted_iota(jnp.int32, (tm, 128), 1)
    low = lane < POOL_GDIM
    sums = jnp.concatenate([jnp.where(low, a2[HALO:, :128], a4[HALO:, :128]),
                            jnp.where(low, a8[HALO:], a16[HALO:])], axis=1)
    col = lax.broadcasted_iota(jnp.int32, (tm, POOL_WIDTH), 1)
    win = jnp.left_shift(2, col // POOL_GDIM)
    t1 = seq_tile * tm + lax.broadcasted_iota(jnp.int32, (tm, POOL_WIDTH), 0) + 1
    pooled = sums / jnp.minimum(t1, win).astype(F32) - p
    mixed = jnp.dot(pooled.astype(BF16), wbd_ref[...], preferred_element_type=F32) * pscale_ref[...]
    pc_ref[:, :POOL_WIDTH] = mixed.astype(BF16)

    y = (convw_ref[0:1, :] * pltpu.roll(u_ext, 2, axis=0)[HALO:]
         + convw_ref[1:2, :] * pltpu.roll(u_ext, 1, axis=0)[HALO:]
         + convw_ref[2:3, :] * u)
    pc_ref[:, POOL_WIDTH:] = (gb * y).astype(BF16)


def _layer_spec(shape, layer, grid_rank):
    index_map = {1: lambda i: (layer, 0, 0), 2: lambda h, b: (layer, 0, 0)}[grid_rank]
    return pl.BlockSpec((None,) + shape, index_map, pipeline_mode=pl.Buffered(1))


def _mix_in(x2d, g, w_in, wbd, pscale, convw, layer):
    t = x2d.shape[0]
    tm = TM_IN
    const = lambda shape: _layer_spec(shape, layer, 1)
    row = lambda width: pl.BlockSpec((tm, width), lambda i: (i, 0))
    return pl.pallas_call(
        _mix_in_kernel,
        grid=(t // tm,),
        in_specs=[row(D_MODEL), const((1, D_MODEL)), const((D_MODEL, IN_COLS)),
                  const((POOL_WIDTH, POOL_WIDTH)), const((1, POOL_WIDTH)), const((CONV_K, CONV_WIDTH))],
        out_specs=[pl.BlockSpec((None, ATT_WIDTH, 2 * tm), lambda i: (i, 0, 0)), row(ATT_WIDTH),
                   pl.BlockSpec((None, ATT_HEADS * VT_ROWS, tm), lambda i: (i, 0, 0)), row(ATT_WIDTH)],
        out_shape=[jax.ShapeDtypeStruct((t // tm, ATT_WIDTH, 2 * tm), BF16),
                   jax.ShapeDtypeStruct((t, ATT_WIDTH), BF16),
                   jax.ShapeDtypeStruct((t // tm, ATT_HEADS * VT_ROWS, tm), BF16),
                   jax.ShapeDtypeStruct((t, ATT_WIDTH), BF16)],
        scratch_shapes=[pltpu.VMEM((HALO, POOL_WIDTH + CONV_WIDTH), F32)],
        compiler_params=pltpu.CompilerParams(dimension_semantics=("arbitrary",),
                                             vmem_limit_bytes=VMEM_LIMIT),
        name="mix_in",
    )(x2d, g, w_in, wbd, pscale, convw)


def _attn_kernel(lq1_ref, lk1_ref, lq2_ref, lk2_ref, g_ref, qs_ref, k_ref, vt_ref, bias_ref, corner_ref, o_ref,
                 *scratch, lam_init):
    n_tiles = qs_ref.shape[0]
    assert n_tiles == N_TILES
    heads = range(HEADS_PER_STEP)
    per_head = [scratch[hd * 8:(hd + 1) * 8] for hd in heads]
    buf_a = [(sc[0], sc[3]) for sc in per_head]
    buf_b = [(sc[1], sc[4]) for sc in per_head]
    buf_c = [(sc[2], sc[5]) for sc in per_head]
    m_refs = [sc[6] for sc in per_head]
    acc_refs = [sc[7] for sc in per_head]
    lam = (jnp.exp(jnp.sum(lq1_ref[...] * lk1_ref[...], axis=-1, keepdims=True))
           - jnp.exp(jnp.sum(lq2_ref[...] * lk2_ref[...], axis=-1, keepdims=True)) + lam_init)

    def write_out(tile_index):
        for hd in heads:
            out = acc_refs[hd][:V_DIM] / acc_refs[hd][V_DIM:V_DIM + 1]
            att = jnp.concatenate([out[:, 0:HALF] - lam * out[:, HALF:TQ],
                                   out[:, TQ:TQ + HALF] - lam * out[:, TQ + HALF:]], axis=1)
            y = (att * lax.rsqrt(jnp.mean(att * att, axis=0, keepdims=True) + SUBLN_EPS)) * g_ref[...]
            o_ref[pl.ds(pl.multiple_of(tile_index * TQ, TQ), TQ), hd * V_DIM:(hd + 1) * V_DIM] = (
                (y * (1.0 - lam_init)).T.astype(BF16))

    def tile(i, carry):
        def reset():
            for hd in heads:
                m_refs[hd][...] = jnp.full_like(m_refs[hd], NEG)
                acc_refs[hd][...] = jnp.zeros_like(acc_refs[hd])

        def key_rows(kind, c):
            return HALF if (kind == DIAGONAL and c == 0) else TQ

        def scores_chunk(hd, c, j, bufs, kind, qtile):
            s_ref, mx_ref = bufs[hd]
            cols = slice(hd * V_DIM, (hd + 1) * V_DIM)
            cc = slice(c * TQ, (c + 1) * TQ)
            nk = key_rows(kind, c)
            kb = k_ref[pl.ds(pl.multiple_of(j * TQ, TQ), nk), cols]
            s = jnp.dot(kb, qs_ref[qtile, cols, cc], preferred_element_type=F32)
            if kind == DIAGONAL:
                s = s + bias_ref[hd, :nk, cc]
            if kind == BELOW_DIAGONAL and c == 0:
                top, bottom = s[:TQ - CORNER], s[TQ - CORNER:]
                corner = corner_ref[hd]
                bottom = jnp.concatenate(
                    [bottom[:, :CORNER] + corner, bottom[:, CORNER:HALF],
                     bottom[:, HALF:HALF + CORNER] + corner, bottom[:, HALF + CORNER:]], axis=1)
                s_ref[:TQ - CORNER, cc] = top
                s_ref[TQ - CORNER:, cc] = bottom
                mx_ref[:, cc] = jnp.maximum(jnp.max(top, axis=0, keepdims=True),
                                            jnp.max(bottom, axis=0, keepdims=True))
                return
            s_ref[:nk, cc] = s
            mx_ref[:, cc] = jnp.max(s, axis=0, keepdims=True)

        def absorb_chunk(hd, c, j, bufs, kind):
            s_ref, mx_ref = bufs[hd]
            cc = slice(c * TQ, (c + 1) * TQ)
            nk = key_rows(kind, c)
            m_old = m_refs[hd][:, cc]
            m_new = jnp.maximum(m_old, mx_ref[:, cc])
            p = jnp.exp2(s_ref[:nk, cc] - m_new).astype(BF16)
            alpha = jnp.exp2(m_old - m_new)
            acc_refs[hd][:, cc] = alpha * acc_refs[hd][:, cc] + jnp.dot(
                vt_ref[j, hd * VT_ROWS:(hd + 1) * VT_ROWS, :nk], p, preferred_element_type=F32)
            m_refs[hd][:, cc] = m_new

        def fused(score=None, absorb=None):
            for hd in heads:
                for c in range(2):
                    if score is not None:
                        scores_chunk(hd, c, *score)
                    if absorb is not None:
                        absorb_chunk(hd, c, *absorb)

        def run(j, nsteps, pending, free):
            for step in range(nsteps):
                fused(score=(j - step - 1, free, PLAIN, i), absorb=(j - step, pending, PLAIN))
                pending, free = free, pending
            return pending

        def finish(pending, kind=PLAIN):
            nxt = jnp.minimum(i + 1, n_tiles - 1)
            fused(score=(nxt, buf_c, DIAGONAL, nxt), absorb=(0, pending, kind))

        @pl.when(i == 0)
        def _():
            reset()
            fused(score=(0, buf_a, DIAGONAL, i))
            finish(buf_a, DIAGONAL)

        for t in range(1, N_TILES):
            @pl.when(i == t)
            def _():
                write_out(i - 1)
                reset()
                fused(score=(i - 1, buf_b, BELOW_DIAGONAL, i), absorb=(i, buf_c, DIAGONAL))
                finish(run(t - 1, t - 1, buf_b, buf_a))

        return carry

    lax.fori_loop(0, n_tiles, tile, 0)
    write_out(n_tiles - 1)


def _attn(qs, k, vt, bias, lq1, lk1, lq2, lk2, subln_g, lam_init, batch, layer):
    t = k.shape[0]
    nq = SEQ // TQ
    hps = HEADS_PER_STEP
    per_head_scratch = ([pltpu.VMEM((TQ, 2 * TQ), F32)] * 3 + [pltpu.VMEM((1, 2 * TQ), F32)] * 4
                        + [pltpu.VMEM((VT_ROWS, 2 * TQ), F32)])
    return pl.pallas_call(
        functools.partial(_attn_kernel, lam_init=lam_init),
        grid=(ATT_HEADS // hps, batch),
        in_specs=[_layer_spec((1, QK_DIM), layer, 2)] * 4 + [
            _layer_spec((V_DIM, 1), layer, 2),
            pl.BlockSpec((nq, hps * V_DIM, 2 * TQ), lambda h, b: (b, h, 0)),
            pl.BlockSpec((SEQ, hps * V_DIM), lambda h, b: (b, h)),
            pl.BlockSpec((nq, hps * VT_ROWS, TQ), lambda h, b: (b, h, 0)),
            pl.BlockSpec((hps, TQ, 2 * TQ), lambda h, b: (h, 0, 0), pipeline_mode=pl.Buffered(1)),
            pl.BlockSpec((hps, CORNER, CORNER), lambda h, b: (h, 0, 0), pipeline_mode=pl.Buffered(1))],
        out_specs=pl.BlockSpec((SEQ, hps * V_DIM), lambda h, b: (b, h)),
        out_shape=jax.ShapeDtypeStruct((t, ATT_WIDTH), BF16),
        scratch_shapes=per_head_scratch * hps,
        compiler_params=pltpu.CompilerParams(dimension_semantics=("arbitrary",) * 2,
                                             vmem_limit_bytes=VMEM_LIMIT),
        name="attn",
    )(lq1, lk1, lq2, lk2, subln_g, qs, k, vt, *bias)


def _out_ffn_kernel(x_ref, att_ref, pc_ref, wo_ref, gffn_ref, wg_ref, wu_ref, wd_ref,
                    gfin_ref, o_ref, *, final):
    y = (jnp.dot(att_ref[...], wo_ref[:ATT_WIDTH], preferred_element_type=F32)
         + jnp.dot(pc_ref[...], wo_ref[ATT_WIDTH:], preferred_element_type=F32))
    x1 = x_ref[...] + y
    h = _rms(x1, gffn_ref[...], EPS).astype(BF16)
    gate = jnp.dot(h, wg_ref[...], preferred_element_type=F32)
    up = jnp.dot(h, wu_ref[...], preferred_element_type=F32)
    a = (jax.nn.silu(gate) * up).astype(BF16)
    x2 = x1 + jnp.dot(a, wd_ref[...], preferred_element_type=F32)
    o_ref[...] = _rms(x2, gfin_ref[...], EPS) if final else x2


def _out_ffn(x2d, att, pc, wo, gffn, wg, wu, wd, gfin, layer, final):
    t = x2d.shape[0]
    tm = TM_FFN
    const = lambda shape: _layer_spec(shape, layer, 1)
    row = lambda width: pl.BlockSpec((tm, width), lambda i: (i, 0))
    return pl.pallas_call(
        functools.partial(_out_ffn_kernel, final=final),
        grid=(t // tm,),
        in_specs=[row(D_MODEL), row(ATT_WIDTH), row(ATT_WIDTH),
                  const((D_MODEL, D_MODEL)), const((1, D_MODEL)),
                  const((D_MODEL, D_FF)), const((D_MODEL, D_FF)), const((D_FF, D_MODEL)),
                  pl.BlockSpec((1, D_MODEL), lambda i: (0, 0))],
        out_specs=row(D_MODEL),
        out_shape=jax.ShapeDtypeStruct((t, D_MODEL), F32),
        compiler_params=pltpu.CompilerParams(dimension_semantics=("arbitrary",),
                                             vmem_limit_bytes=VMEM_LIMIT),
        name="out_ffn",
    )(x2d, att, pc, wo, gffn, wg, wu, wd, gfin)


def _block_diag(w_pool):
    depth, groups = w_pool.shape[:2]
    out = jnp.zeros((depth, POOL_WIDTH, POOL_WIDTH), w_pool.dtype)
    for g in range(groups):
        out = out.at[:, g * POOL_GDIM:(g + 1) * POOL_GDIM, g * POOL_GDIM:(g + 1) * POOL_GDIM].set(w_pool[:, g])
    return out


def kernel(x, g_mix, w_in, lambda_q1, lambda_k1, lambda_q2, lambda_k2, subln_g, rel_bias,
           w_pool, pool_scale, conv_w, w_o, g_ffn, w_gate, w_up, w_down, g_final):
    batch, seq, d = x.shape
    assert (seq, d) == (SEQ, D_MODEL)
    depth = w_in.shape[0]
    xf = x.reshape(batch * seq, d)
    bias = _bias_tiles(rel_bias.astype(F32))
    w_in_b, w_o_b, w_gate_b, w_up_b, w_down_b = (w.astype(BF16) for w in (w_in, w_o, w_gate, w_up, w_down))
    wbd = _block_diag(w_pool).astype(BF16)
    as_rows = lambda v: v[:, None, :]
    for l in range(depth):
        lam_init = 0.8 - 0.6 * math.exp(-0.3 * l)
        qs, k, vt, pc = _mix_in(xf, as_rows(g_mix), w_in_b, wbd, as_rows(pool_scale), conv_w, l)
        att = _attn(qs, k, vt, bias, as_rows(lambda_q1), as_rows(lambda_k1), as_rows(lambda_q2),
                    as_rows(lambda_k2), subln_g[:, :, None], lam_init, batch, l)
        xf = _out_ffn(xf, att, pc, w_o_b, as_rows(g_ffn), w_gate_b, w_up_b, w_down_b,
                      g_final[None], l, final=(l == depth - 1))
    return xf.reshape(batch, seq, d)
```

```python
import functools
import math

import numpy as np
import jax
import jax.numpy as jnp
from jax import lax
from jax.experimental import pallas as pl
from jax.experimental.pallas import tpu as pltpu

F32 = jnp.float32
BF16 = jnp.bfloat16

D_MODEL = 1024
SEQ = 4096
ATT_HEADS = 4
QK_DIM = 64
V_DIM = 2 * QK_DIM
ATT_WIDTH = ATT_HEADS * V_DIM
BF16_ROWS = 16
VT_ROWS = V_DIM + BF16_ROWS
POOL_WINDOWS = (2, 4, 8, 16)
POOL_WIDTH = 256
POOL_GDIM = 64
CONV_WIDTH = 256
CONV_K = 3
IN_COLS = 3 * ATT_WIDTH + POOL_WIDTH + 3 * CONV_WIDTH
D_FF = 2816
NUM_BUCKETS = 32
MAX_EXACT = 16
MAX_DISTANCE = 128
EPS = 1e-6
SUBLN_EPS = 1e-5
NEG = -1e30
LOG2E = math.log2(math.e)

HALO = 16
TQ = 512
HALF = TQ // 2
CORNER = 128
PLAIN, BELOW_DIAGONAL, DIAGONAL = range(3)
HEADS_PER_STEP = 2
N_TILES = SEQ // TQ
TM_IN = TQ
TM_FFN = 512
VMEM_LIMIT = 56 * 1024 * 1024


def _bucket_thresholds():
    n = np.arange(MAX_DISTANCE)
    nf = np.maximum(n, 1).astype(np.float32)
    large = MAX_EXACT + (np.log(nf / MAX_EXACT) / math.log(MAX_DISTANCE / MAX_EXACT)
                         * (NUM_BUCKETS - MAX_EXACT)).astype(np.int32)
    bucket = np.where(n < MAX_EXACT, n, np.minimum(large, NUM_BUCKETS - 1))
    assert bucket[-1] == NUM_BUCKETS - 1 and np.all(np.diff(bucket) >= 0)
    return [int(np.argmax(bucket >= b)) for b in range(NUM_BUCKETS)]


_BUCKET_THR = _bucket_thresholds()


def _rms(x, g, eps):
    return (x * lax.rsqrt(jnp.mean(x * x, axis=-1, keepdims=True) + eps)) * g


def _bias_value(relb_ref, h, dist):
    val = jnp.full(dist.shape, relb_ref[0, h], F32)
    for b in range(1, NUM_BUCKETS):
        val = jnp.where(dist >= _BUCKET_THR[b], relb_ref[b, h], val)
    return jnp.where(dist >= 0, (val - relb_ref[NUM_BUCKETS - 1, h]) * LOG2E, NEG)


def _bias_kernel(relb_ref, diag_ref, corner_ref):
    h = pl.program_id(0)
    kpos = lax.broadcasted_iota(jnp.int32, (TQ, TQ), 0)
    qpos = lax.broadcasted_iota(jnp.int32, (TQ, TQ), 1)
    val = _bias_value(relb_ref, h, qpos - kpos)
    for chunk in range(4):
        diag_ref[:, chunk * HALF:(chunk + 1) * HALF] = val[:, (chunk // 2) * HALF:(chunk // 2 + 1) * HALF]
    kpos = lax.broadcasted_iota(jnp.int32, (CORNER, CORNER), 0) + (TQ - CORNER)
    qpos = lax.broadcasted_iota(jnp.int32, (CORNER, CORNER), 1)
    corner_ref[...] = _bias_value(relb_ref, h, TQ + qpos - kpos)


def _bias_tiles(rel_bias):
    assert CORNER >= MAX_DISTANCE and _BUCKET_THR[NUM_BUCKETS - 1] <= MAX_DISTANCE
    return pl.pallas_call(
        _bias_kernel,
        grid=(ATT_HEADS,),
        in_specs=[pl.BlockSpec(memory_space=pltpu.SMEM)],
        out_specs=[pl.BlockSpec((None, TQ, 2 * TQ), lambda h: (h, 0, 0)),
                   pl.BlockSpec((None, CORNER, CORNER), lambda h: (h, 0, 0))],
        out_shape=[jax.ShapeDtypeStruct((ATT_HEADS, TQ, 2 * TQ), F32),
                   jax.ShapeDtypeStruct((ATT_HEADS, CORNER, CORNER), F32)],
        name="bias_tiles",
    )(rel_bias)


def _mix_in_kernel(x_ref, g_ref, w_ref, wbd_ref, pscale_ref, convw_ref,
                   qs_ref, k_ref, vt_ref, pc_ref, carry_ref):
    tm = x_ref.shape[0]
    tiles_per_seq = SEQ // tm
    seq_tile = pl.program_id(0) % tiles_per_seq

    @pl.when(seq_tile == 0)
    def _():
        carry_ref[...] = jnp.zeros_like(carry_ref)

    h = _rms(x_ref[...], g_ref[...], EPS).astype(BF16)
    o = 3 * ATT_WIDTH
    proj = jnp.dot(h, w_ref[:, o:], preferred_element_type=F32)
    p = proj[:, :POOL_WIDTH]
    gb = proj[:, POOL_WIDTH:POOL_WIDTH + CONV_WIDTH]
    gc = proj[:, POOL_WIDTH + CONV_WIDTH:POOL_WIDTH + 2 * CONV_WIDTH]
    hin = proj[:, POOL_WIDTH + 2 * CONV_WIDTH:]
    u = gc * hin

    qkv = jnp.dot(h, w_ref[:, :o], preferred_element_type=F32)
    qt = (qkv[:, 0:ATT_WIDTH] * (QK_DIM ** -0.5 * LOG2E)).T.astype(BF16)
    first_map = lax.broadcasted_iota(jnp.int32, qt.shape, 0) % V_DIM < QK_DIM
    zero = jnp.zeros_like(qt)
    maps = (jnp.where(first_map, qt, zero), jnp.where(first_map, zero, qt))
    for chunk in range(4):
        qs_ref[:, chunk * HALF:(chunk + 1) * HALF] = maps[chunk % 2][:, (chunk // 2) * HALF:(chunk // 2 + 1) * HALF]
    k_ref[...] = qkv[:, ATT_WIDTH:2 * ATT_WIDTH].astype(BF16)
    vt = qkv[:, 2 * ATT_WIDTH:].T.astype(BF16)
    ones = jnp.ones((BF16_ROWS, tm), BF16)
    for hd in range(ATT_HEADS):
        vt_ref[hd * VT_ROWS:hd * VT_ROWS + V_DIM] = vt[hd * V_DIM:(hd + 1) * V_DIM]
        vt_ref[hd * VT_ROWS + V_DIM:(hd + 1) * VT_ROWS] = ones

    p_ext = jnp.concatenate([carry_ref[:, :POOL_WIDTH], p], axis=0)
    u_ext = jnp.concatenate([carry_ref[:, POOL_WIDTH:], u], axis=0)
    carry_ref[:, :POOL_WIDTH] = p[tm - HALO:]
    carry_ref[:, POOL_WIDTH:] = u[tm - HALO:]

    a2 = p_ext + pltpu.roll(p_ext, 1, axis=0)
    a4 = a2 + pltpu.roll(a2, 2, axis=0)
    a4_hi = a4[:, 128:]
    a8 = a4_hi + pltpu.roll(a4_hi, 4, axis=0)
    a16 = a8 + pltpu.roll(a8, 8, axis=0)
    lane = lax.broadcasted_iota(jnp.int32, (tm, 128), 1)
    low = lane < POOL_GDIM
    sums = jnp.concatenate([jnp.where(low, a2[HALO:, :128], a4[HALO:, :128]),
                            jnp.where(low, a8[HALO:], a16[HALO:])], axis=1)
    col = lax.broadcasted_iota(jnp.int32, (tm, POOL_WIDTH), 1)
    win = jnp.left_shift(2, col // POOL_GDIM)
    t1 = seq_tile * tm + lax.broadcasted_iota(jnp.int32, (tm, POOL_WIDTH), 0) + 1
    pooled = sums / jnp.minimum(t1, win).astype(F32) - p
    mixed = jnp.dot(pooled.astype(BF16), wbd_ref[...], preferred_element_type=F32) * pscale_ref[...]
    pc_ref[:, :POOL_WIDTH] = mixed.astype(BF16)

    y = (convw_ref[0:1, :] * pltpu.roll(u_ext, 2, axis=0)[HALO:]
         + convw_ref[1:2, :] * pltpu.roll(u_ext, 1, axis=0)[HALO:]
         + convw_ref[2:3, :] * u)
    pc_ref[:, POOL_WIDTH:] = (gb * y).astype(BF16)


def _layer_spec(shape, layer, grid_rank):
    index_map = {1: lambda i: (layer, 0, 0), 2: lambda h, b: (layer, 0, 0)}[grid_rank]
    return pl.BlockSpec((None,) + shape, index_map, pipeline_mode=pl.Buffered(1))


def _mix_in(x2d, g, w_in, wbd, pscale, convw, layer):
    t = x2d.shape[0]
    tm = TM_IN
    const = lambda shape: _layer_spec(shape, layer, 1)
    row = lambda width: pl.BlockSpec((tm, width), lambda i: (i, 0))
    return pl.pallas_call(
        _mix_in_kernel,
        grid=(t // tm,),
        in_specs=[row(D_MODEL), const((1, D_MODEL)), const((D_MODEL, IN_COLS)),
                  const((POOL_WIDTH, POOL_WIDTH)), const((1, POOL_WIDTH)), const((CONV_K, CONV_WIDTH))],
        out_specs=[pl.BlockSpec((None, ATT_WIDTH, 2 * tm), lambda i: (i, 0, 0)), row(ATT_WIDTH),
                   pl.BlockSpec((None, ATT_HEADS * VT_ROWS, tm), lambda i: (i, 0, 0)), row(ATT_WIDTH)],
        out_shape=[jax.ShapeDtypeStruct((t // tm, ATT_WIDTH, 2 * tm), BF16),
                   jax.ShapeDtypeStruct((t, ATT_WIDTH), BF16),
                   jax.ShapeDtypeStruct((t // tm, ATT_HEADS * VT_ROWS, tm), BF16),
                   jax.ShapeDtypeStruct((t, ATT_WIDTH), BF16)],
        scratch_shapes=[pltpu.VMEM((HALO, POOL_WIDTH + CONV_WIDTH), F32)],
        compiler_params=pltpu.CompilerParams(dimension_semantics=("arbitrary",),
                                             vmem_limit_bytes=VMEM_LIMIT),
        name="mix_in",
    )(x2d, g, w_in, wbd, pscale, convw)


def _attn_kernel(lq1_ref, lk1_ref, lq2_ref, lk2_ref, g_ref, qs_ref, k_ref, vt_ref, bias_ref, corner_ref, o_ref,
                 *scratch, lam_init):
    n_tiles = qs_ref.shape[0]
    assert n_tiles == N_TILES
    heads = range(HEADS_PER_STEP)
    per_head = [scratch[hd * 8:(hd + 1) * 8] for hd in heads]
    buf_a = [(sc[0], sc[3]) for sc in per_head]
    buf_b = [(sc[1], sc[4]) for sc in per_head]
    buf_c = [(sc[2], sc[5]) for sc in per_head]
    m_refs = [sc[6] for sc in per_head]
    acc_refs = [sc[7] for sc in per_head]
    lam = (jnp.exp(jnp.sum(lq1_ref[...] * lk1_ref[...], axis=-1, keepdims=True))
           - jnp.exp(jnp.sum(lq2_ref[...] * lk2_ref[...], axis=-1, keepdims=True)) + lam_init)

    def write_out(tile_index):
        for hd in heads:
            out = acc_refs[hd][:V_DIM] / acc_refs[hd][V_DIM:V_DIM + 1]
            att = jnp.concatenate([out[:, 0:HALF] - lam * out[:, HALF:TQ],
                                   out[:, TQ:TQ + HALF] - lam * out[:, TQ + HALF:]], axis=1)
            y = (att * lax.rsqrt(jnp.mean(att * att, axis=0, keepdims=True) + SUBLN_EPS)) * g_ref[...]
            o_ref[pl.ds(pl.multiple_of(tile_index * TQ, TQ), TQ), hd * V_DIM:(hd + 1) * V_DIM] = (
                (y * (1.0 - lam_init)).T.astype(BF16))

    def tile(i, carry):
        def reset():
            for hd in heads:
                m_refs[hd][...] = jnp.full_like(m_refs[hd], NEG)
                acc_refs[hd][...] = jnp.zeros_like(acc_refs[hd])

        def key_rows(kind, c):
            return HALF if (kind == DIAGONAL and c == 0) else TQ

        def scores_chunk(hd, c, j, bufs, kind, qtile):
            s_ref, mx_ref = bufs[hd]
            cols = slice(hd * V_DIM, (hd + 1) * V_DIM)
            cc = slice(c * TQ, (c + 1) * TQ)
            nk = key_rows(kind, c)
            kb = k_ref[pl.ds(pl.multiple_of(j * TQ, TQ), nk), cols]
            s = jnp.dot(kb, qs_ref[qtile, cols, cc], preferred_element_type=F32)
            if kind == DIAGONAL:
                s = s + bias_ref[hd, :nk, cc]
            if kind == BELOW_DIAGONAL and c == 0:
                top, bottom = s[:TQ - CORNER], s[TQ - CORNER:]
                corner = corner_ref[hd]
                bottom = jnp.concatenate(
                    [bottom[:, :CORNER] + corner, bottom[:, CORNER:HALF],
                     bottom[:, HALF:HALF + CORNER] + corner, bottom[:, HALF + CORNER:]], axis=1)
                s_ref[:TQ - CORNER, cc] = top
                s_ref[TQ - CORNER:, cc] = bottom
                mx_ref[:, cc] = jnp.maximum(jnp.max(top, axis=0, keepdims=True),
                                            jnp.max(bottom, axis=0, keepdims=True))
                return
            s_ref[:nk, cc] = s
            mx_ref[:, cc] = jnp.max(s, axis=0, keepdims=True)

        def absorb_chunk(hd, c, j, bufs, kind):
            s_ref, mx_ref = bufs[hd]
            cc = slice(c * TQ, (c + 1) * TQ)
            nk = key_rows(kind, c)
            m_old = m_refs[hd][:, cc]
            m_new = jnp.maximum(m_old, mx_ref[:, cc])
            p = jnp.exp2(s_ref[:nk, cc] - m_new).astype(BF16)
            alpha = jnp.exp2(m_old - m_new)
            acc_refs[hd][:, cc] = alpha * acc_refs[hd][:, cc] + jnp.dot(
                vt_ref[j, hd * VT_ROWS:(hd + 1) * VT_ROWS, :nk], p, preferred_element_type=F32)
            m_refs[hd][:, cc] = m_new

        def fused(score=None, absorb=None):
            for hd in heads:
                for c in range(2):
                    if score is not None:
                        scores_chunk(hd, c, *score)
                    if absorb is not None:
                        absorb_chunk(hd, c, *absorb)

        def run(j, nsteps, pending, free):
            for step in range(nsteps):
                fused(score=(j - step - 1, free, PLAIN, i), absorb=(j - step, pending, PLAIN))
                pending, free = free, pending
            return pending

        def finish(pending, kind=PLAIN):
            nxt = jnp.minimum(i + 1, n_tiles - 1)
            fused(score=(nxt, buf_c, DIAGONAL, nxt), absorb=(0, pending, kind))

        @pl.when(i == 0)
        def _():
            reset()
            fused(score=(0, buf_a, DIAGONAL, i))
            finish(buf_a, DIAGONAL)

        for t in range(1, N_TILES):
            @pl.when(i == t)
            def _():
                write_out(i - 1)
                reset()
                fused(score=(i - 1, buf_b, BELOW_DIAGONAL, i), absorb=(i, buf_c, DIAGONAL))
                finish(run(t - 1, t - 1, buf_b, buf_a))

        return carry

    lax.fori_loop(0, n_tiles, tile, 0)
    write_out(n_tiles - 1)


def _attn(qs, k, vt, bias, lq1, lk1, lq2, lk2, subln_g, lam_init, batch, layer):
    t = k.shape[0]
    nq = SEQ // TQ
    hps = HEADS_PER_STEP
    per_head_scratch = ([pltpu.VMEM((TQ, 2 * TQ), F32)] * 3 + [pltpu.VMEM((1, 2 * TQ), F32)] * 4
                        + [pltpu.VMEM((VT_ROWS, 2 * TQ), F32)])
    return pl.pallas_call(
        functools.partial(_attn_kernel, lam_init=lam_init),
        grid=(ATT_HEADS // hps, batch),
        in_specs=[_layer_spec((1, QK_DIM), layer, 2)] * 4 + [
            _layer_spec((V_DIM, 1), layer, 2),
            pl.BlockSpec((nq, hps * V_DIM, 2 * TQ), lambda h, b: (b, h, 0)),
            pl.BlockSpec((SEQ, hps * V_DIM), lambda h, b: (b, h)),
            pl.BlockSpec((nq, hps * VT_ROWS, TQ), lambda h, b: (b, h, 0)),
            pl.BlockSpec((hps, TQ, 2 * TQ), lambda h, b: (h, 0, 0), pipeline_mode=pl.Buffered(1)),
            pl.BlockSpec((hps, CORNER, CORNER), lambda h, b: (h, 0, 0), pipeline_mode=pl.Buffered(1))],
        out_specs=pl.BlockSpec((SEQ, hps * V_DIM), lambda h, b: (b, h)),
        out_shape=jax.ShapeDtypeStruct((t, ATT_WIDTH), BF16),
        scratch_shapes=per_head_scratch * hps,
        compiler_params=pltpu.CompilerParams(dimension_semantics=("arbitrary",) * 2,
                                             vmem_limit_bytes=VMEM_LIMIT),
        name="attn",
    )(lq1, lk1, lq2, lk2, subln_g, qs, k, vt, *bias)


def _out_ffn_kernel(x_ref, att_ref, pc_ref, wo_ref, gffn_ref, wg_ref, wu_ref, wd_ref,
                    gfin_ref, o_ref, *, final):
    tm = x_ref.shape[0]
    halves = [slice(r * (tm // 2), (r + 1) * (tm // 2)) for r in range(2)]
    x1 = [x_ref[rows, :] + (jnp.dot(att_ref[rows, :], wo_ref[:ATT_WIDTH], preferred_element_type=F32)
                            + jnp.dot(pc_ref[rows, :], wo_ref[ATT_WIDTH:], preferred_element_type=F32))
          for rows in halves]
    h = [_rms(v, gffn_ref[...], EPS).astype(BF16) for v in x1]
    gate = [jnp.dot(v, wg_ref[...], preferred_element_type=F32) for v in h]
    up = [jnp.dot(v, wu_ref[...], preferred_element_type=F32) for v in h]
    a = [(jax.nn.silu(g) * u).astype(BF16) for g, u in zip(gate, up)]
    for rows, v1, va in zip(halves, x1, a):
        x2 = v1 + jnp.dot(va, wd_ref[...], preferred_element_type=F32)
        o_ref[rows, :] = _rms(x2, gfin_ref[...], EPS) if final else x2


def _out_ffn(x2d, att, pc, wo, gffn, wg, wu, wd, gfin, layer, final):
    t = x2d.shape[0]
    tm = TM_FFN
    const = lambda shape: _layer_spec(shape, layer, 1)
    row = lambda width: pl.BlockSpec((tm, width), lambda i: (i, 0))
    return pl.pallas_call(
        functools.partial(_out_ffn_kernel, final=final),
        grid=(t // tm,),
        in_specs=[row(D_MODEL), row(ATT_WIDTH), row(ATT_WIDTH),
                  const((D_MODEL, D_MODEL)), const((1, D_MODEL)),
                  const((D_MODEL, D_FF)), const((D_MODEL, D_FF)), const((D_FF, D_MODEL)),
                  pl.BlockSpec((1, D_MODEL), lambda i: (0, 0))],
        out_specs=row(D_MODEL),
        out_shape=jax.ShapeDtypeStruct((t, D_MODEL), F32),
        compiler_params=pltpu.CompilerParams(dimension_semantics=("arbitrary",),
                                             vmem_limit_bytes=VMEM_LIMIT),
        name="out_ffn",
    )(x2d, att, pc, wo, gffn, wg, wu, wd, gfin)


def _block_diag(w_pool):
    depth, groups = w_pool.shape[:2]
    out = jnp.zeros((depth, POOL_WIDTH, POOL_WIDTH), w_pool.dtype)
    for g in range(groups):
        out = out.at[:, g * POOL_GDIM:(g + 1) * POOL_GDIM, g * POOL_GDIM:(g + 1) * POOL_GDIM].set(w_pool[:, g])
    return out


def kernel(x, g_mix, w_in, lambda_q1, lambda_k1, lambda_q2, lambda_k2, subln_g, rel_bias,
           w_pool, pool_scale, conv_w, w_o, g_ffn, w_gate, w_up, w_down, g_final):
    batch, seq, d = x.shape
    assert (seq, d) == (SEQ, D_MODEL)
    depth = w_in.shape[0]
    xf = x.reshape(batch * seq, d)
    bias = _bias_tiles(rel_bias.astype(F32))
    w_in_b, w_o_b, w_gate_b, w_up_b, w_down_b = (w.astype(BF16) for w in (w_in, w_o, w_gate, w_up, w_down))
    wbd = _block_diag(w_pool).astype(BF16)
    as_rows = lambda v: v[:, None, :]
    for l in range(depth):
        lam_init = 0.8 - 0.6 * math.exp(-0.3 * l)
        qs, k, vt, pc = _mix_in(xf, as_rows(g_mix), w_in_b, wbd, as_rows(pool_scale), conv_w, l)
        att = _attn(qs, k, vt, bias, as_rows(lambda_q1), as_rows(lambda_k1), as_rows(lambda_q2),
                    as_rows(lambda_k2), subln_g[:, :, None], lam_init, batch, l)
        xf = _out_ffn(xf, att, pc, w_o_b, as_rows(g_ffn), w_gate_b, w_up_b, w_down_b,
                      g_final[None], l, final=(l == depth - 1))
    return xf.reshape(batch, seq, d)
```

```python
import functools
import math

import numpy as np
import jax
import jax.numpy as jnp
from jax import lax
from jax.experimental import pallas as pl
from jax.experimental.pallas import tpu as pltpu

F32 = jnp.float32
BF16 = jnp.bfloat16

D_MODEL = 1024
SEQ = 4096
ATT_HEADS = 4
QK_DIM = 64
V_DIM = 2 * QK_DIM
ATT_WIDTH = ATT_HEADS * V_DIM
BF16_ROWS = 16
VT_ROWS = V_DIM + BF16_ROWS
POOL_WINDOWS = (2, 4, 8, 16)
POOL_WIDTH = 256
POOL_GDIM = 64
CONV_WIDTH = 256
CONV_K = 3
IN_COLS = 3 * ATT_WIDTH + POOL_WIDTH + 3 * CONV_WIDTH
D_FF = 2816
NUM_BUCKETS = 32
MAX_EXACT = 16
MAX_DISTANCE = 128
EPS = 1e-6
SUBLN_EPS = 1e-5
NEG = -1e30
LOG2E = math.log2(math.e)

HALO = 16
TQ = 512
HALF = TQ // 2
CORNER = 128
PLAIN, BELOW_DIAGONAL, DIAGONAL = range(3)
HEADS_PER_STEP = 2
N_TILES = SEQ // TQ
TM_IN = TQ
TM_FFN = 512
VMEM_LIMIT = 56 * 1024 * 1024


def _bucket_thresholds():
    n = np.arange(MAX_DISTANCE)
    nf = np.maximum(n, 1).astype(np.float32)
    large = MAX_EXACT + (np.log(nf / MAX_EXACT) / math.log(MAX_DISTANCE / MAX_EXACT)
                         * (NUM_BUCKETS - MAX_EXACT)).astype(np.int32)
    bucket = np.where(n < MAX_EXACT, n, np.minimum(large, NUM_BUCKETS - 1))
    assert bucket[-1] == NUM_BUCKETS - 1 and np.all(np.diff(bucket) >= 0)
    return [int(np.argmax(bucket >= b)) for b in range(NUM_BUCKETS)]


_BUCKET_THR = _bucket_thresholds()


def _rms(x, g, eps):
    return (x * lax.rsqrt(jnp.mean(x * x, axis=-1, keepdims=True) + eps)) * g


def _bias_value(relb_ref, h, dist):
    val = jnp.full(dist.shape, relb_ref[0, h], F32)
    for b in range(1, NUM_BUCKETS):
        val = jnp.where(dist >= _BUCKET_THR[b], relb_ref[b, h], val)
    return jnp.where(dist >= 0, (val - relb_ref[NUM_BUCKETS - 1, h]) * LOG2E, NEG)


def _bias_kernel(relb_ref, diag_ref, corner_ref):
    h = pl.program_id(0)
    kpos = lax.broadcasted_iota(jnp.int32, (TQ, TQ), 0)
    qpos = lax.broadcasted_iota(jnp.int32, (TQ, TQ), 1)
    val = _bias_value(relb_ref, h, qpos - kpos)
    for chunk in range(4):
        diag_ref[:, chunk * HALF:(chunk + 1) * HALF] = val[:, (chunk // 2) * HALF:(chunk // 2 + 1) * HALF]
    kpos = lax.broadcasted_iota(jnp.int32, (CORNER, CORNER), 0) + (TQ - CORNER)
    qpos = lax.broadcasted_iota(jnp.int32, (CORNER, CORNER), 1)
    corner_ref[...] = _bias_value(relb_ref, h, TQ + qpos - kpos)


def _bias_tiles(rel_bias):
    assert CORNER >= MAX_DISTANCE and _BUCKET_THR[NUM_BUCKETS - 1] <= MAX_DISTANCE
    return pl.pallas_call(
        _bias_kernel,
        grid=(ATT_HEADS,),
        in_specs=[pl.BlockSpec(memory_space=pltpu.SMEM)],
        out_specs=[pl.BlockSpec((None, TQ, 2 * TQ), lambda h: (h, 0, 0)),
                   pl.BlockSpec((None, CORNER, CORNER), lambda h: (h, 0, 0))],
        out_shape=[jax.ShapeDtypeStruct((ATT_HEADS, TQ, 2 * TQ), F32),
                   jax.ShapeDtypeStruct((ATT_HEADS, CORNER, CORNER), F32)],
        name="bias_tiles",
    )(rel_bias)


def _mix_in_kernel(x_ref, g_ref, w_ref, wbd_ref, pscale_ref, convw_ref,
                   qs_ref, k_ref, vt_ref, pc_ref, carry_ref):
    tm = x_ref.shape[0]
    tiles_per_seq = SEQ // tm
    seq_tile = pl.program_id(0) % tiles_per_seq

    @pl.when(seq_tile == 0)
    def _():
        carry_ref[...] = jnp.zeros_like(carry_ref)

    h = _rms(x_ref[...], g_ref[...], EPS).astype(BF16)
    o = 3 * ATT_WIDTH
    proj = jnp.dot(h, w_ref[:, o:], preferred_element_type=F32)
    p = proj[:, :POOL_WIDTH]
    gb = proj[:, POOL_WIDTH:POOL_WIDTH + CONV_WIDTH]
    gc = proj[:, POOL_WIDTH + CONV_WIDTH:POOL_WIDTH + 2 * CONV_WIDTH]
    hin = proj[:, POOL_WIDTH + 2 * CONV_WIDTH:]
    u = gc * hin

    qkv = jnp.dot(h, w_ref[:, :o], preferred_element_type=F32)
    qt = (qkv[:, 0:ATT_WIDTH] * (QK_DIM ** -0.5 * LOG2E)).T.astype(BF16)
    first_map = lax.broadcasted_iota(jnp.int32, qt.shape, 0) % V_DIM < QK_DIM
    zero = jnp.zeros_like(qt)
    maps = (jnp.where(first_map, qt, zero), jnp.where(first_map, zero, qt))
    for chunk in range(4):
        qs_ref[:, chunk * HALF:(chunk + 1) * HALF] = maps[chunk % 2][:, (chunk // 2) * HALF:(chunk // 2 + 1) * HALF]
    k_ref[...] = qkv[:, ATT_WIDTH:2 * ATT_WIDTH].astype(BF16)
    vt = qkv[:, 2 * ATT_WIDTH:].T.astype(BF16)
    ones = jnp.ones((BF16_ROWS, tm), BF16)
    for hd in range(ATT_HEADS):
        vt_ref[hd * VT_ROWS:hd * VT_ROWS + V_DIM] = vt[hd * V_DIM:(hd + 1) * V_DIM]
        vt_ref[hd * VT_ROWS + V_DIM:(hd + 1) * VT_ROWS] = ones

    p_ext = jnp.concatenate([carry_ref[:, :POOL_WIDTH], p], axis=0)
    u_ext = jnp.concatenate([carry_ref[:, POOL_WIDTH:], u], axis=0)
    carry_ref[:, :POOL_WIDTH] = p[tm - HALO:]
    carry_ref[:, POOL_WIDTH:] = u[tm - HALO:]

    a2 = p_ext + pltpu.roll(p_ext, 1, axis=0)
    a4 = a2 + pltpu.roll(a2, 2, axis=0)
    a4_hi = a4[:, 128:]
    a8 = a4_hi + pltpu.roll(a4_hi, 4, axis=0)
    a16 = a8 + pltpu.roll(a8, 8, axis=0)
    lane = lax.broadcasted_iota(jnp.int32, (tm, 128), 1)
    low = lane < POOL_GDIM
    sums = jnp.concatenate([jnp.where(low, a2[HALO:, :128], a4[HALO:, :128]),
                            jnp.where(low, a8[HALO:], a16[HALO:])], axis=1)
    col = lax.broadcasted_iota(jnp.int32, (tm, POOL_WIDTH), 1)
    win = jnp.left_shift(2, col // POOL_GDIM)
    t1 = seq_tile * tm + lax.broadcasted_iota(jnp.int32, (tm, POOL_WIDTH), 0) + 1
    pooled = sums / jnp.minimum(t1, win).astype(F32) - p
    mixed = jnp.dot(pooled.astype(BF16), wbd_ref[...], preferred_element_type=F32) * pscale_ref[...]
    pc_ref[:, :POOL_WIDTH] = mixed.astype(BF16)

    y = (convw_ref[0:1, :] * pltpu.roll(u_ext, 2, axis=0)[HALO:]
         + convw_ref[1:2, :] * pltpu.roll(u_ext, 1, axis=0)[HALO:]
         + convw_ref[2:3, :] * u)
    pc_ref[:, POOL_WIDTH:] = (gb * y).astype(BF16)


def _layer_spec(shape, layer, grid_rank):
    index_map = {1: lambda i: (layer, 0, 0), 2: lambda h, b: (layer, 0, 0)}[grid_rank]
    return pl.BlockSpec((None,) + shape, index_map, pipeline_mode=pl.Buffered(1))


def _mix_in(x2d, g, w_in, wbd, pscale, convw, layer):
    t = x2d.shape[0]
    tm = TM_IN
    const = lambda shape: _layer_spec(shape, layer, 1)
    row = lambda width: pl.BlockSpec((tm, width), lambda i: (i, 0))
    return pl.pallas_call(
        _mix_in_kernel,
        grid=(t // tm,),
        in_specs=[row(D_MODEL), const((1, D_MODEL)), const((D_MODEL, IN_COLS)),
                  const((POOL_WIDTH, POOL_WIDTH)), const((1, POOL_WIDTH)), const((CONV_K, CONV_WIDTH))],
        out_specs=[pl.BlockSpec((None, ATT_WIDTH, 2 * tm), lambda i: (i, 0, 0)), row(ATT_WIDTH),
                   pl.BlockSpec((None, ATT_HEADS * VT_ROWS, tm), lambda i: (i, 0, 0)), row(ATT_WIDTH)],
        out_shape=[jax.ShapeDtypeStruct((t // tm, ATT_WIDTH, 2 * tm), BF16),
                   jax.ShapeDtypeStruct((t, ATT_WIDTH), BF16),
                   jax.ShapeDtypeStruct((t // tm, ATT_HEADS * VT_ROWS, tm), BF16),
                   jax.ShapeDtypeStruct((t, ATT_WIDTH), BF16)],
        scratch_shapes=[pltpu.VMEM((HALO, POOL_WIDTH + CONV_WIDTH), F32)],
        compiler_params=pltpu.CompilerParams(dimension_semantics=("arbitrary",),
                                             vmem_limit_bytes=VMEM_LIMIT),
        name="mix_in",
    )(x2d, g, w_in, wbd, pscale, convw)


def _attn_kernel(lq1_ref, lk1_ref, lq2_ref, lk2_ref, g_ref, qs_ref, k_ref, vt_ref, bias_ref, corner_ref, o_ref,
                 *scratch, lam_init):
    n_tiles = qs_ref.shape[0]
    assert n_tiles == N_TILES
    heads = range(HEADS_PER_STEP)
    per_head = [scratch[hd * 8:(hd + 1) * 8] for hd in heads]
    buf_a = [(sc[0], sc[3]) for sc in per_head]
    buf_b = [(sc[1], sc[4]) for sc in per_head]
    buf_c = [(sc[2], sc[5]) for sc in per_head]
    m_refs = [sc[6] for sc in per_head]
    acc_refs = [sc[7] for sc in per_head]
    lam = (jnp.exp(jnp.sum(lq1_ref[...] * lk1_ref[...], axis=-1, keepdims=True))
           - jnp.exp(jnp.sum(lq2_ref[...] * lk2_ref[...], axis=-1, keepdims=True)) + lam_init)

    def write_out(tile_index):
        for hd in heads:
            out = acc_refs[hd][:V_DIM] / acc_refs[hd][V_DIM:V_DIM + 1]
            att = jnp.concatenate([out[:, 0:HALF] - lam * out[:, HALF:TQ],
                                   out[:, TQ:TQ + HALF] - lam * out[:, TQ + HALF:]], axis=1)
            y = (att * lax.rsqrt(jnp.mean(att * att, axis=0, keepdims=True) + SUBLN_EPS)) * g_ref[...]
            o_ref[pl.ds(pl.multiple_of(tile_index * TQ, TQ), TQ), hd * V_DIM:(hd + 1) * V_DIM] = (
                (y * (1.0 - lam_init)).T.astype(BF16))

    def tile(i, carry):
        def reset():
            for hd in heads:
                m_refs[hd][...] = jnp.full_like(m_refs[hd], NEG)
                acc_refs[hd][...] = jnp.zeros_like(acc_refs[hd])

        def key_rows(kind, c):
            return HALF if (kind == DIAGONAL and c == 0) else TQ

        def scores_chunk(hd, c, j, bufs, kind, qtile):
            s_ref, mx_ref = bufs[hd]
            cols = slice(hd * V_DIM, (hd + 1) * V_DIM)
            cc = slice(c * TQ, (c + 1) * TQ)
            nk = key_rows(kind, c)
            kb = k_ref[pl.ds(pl.multiple_of(j * TQ, TQ), nk), cols]
            s = jnp.dot(kb, qs_ref[qtile, cols, cc], preferred_element_type=F32)
            if kind == DIAGONAL:
                s = s + bias_ref[hd, :nk, cc]
            if kind == BELOW_DIAGONAL and c == 0:
                top, bottom = s[:TQ - CORNER], s[TQ - CORNER:]
                corner = corner_ref[hd]
                bottom = jnp.concatenate(
                    [bottom[:, :CORNER] + corner, bottom[:, CORNER:HALF],
                     bottom[:, HALF:HALF + CORNER] + corner, bottom[:, HALF + CORNER:]], axis=1)
                s_ref[:TQ - CORNER, cc] = top
                s_ref[TQ - CORNER:, cc] = bottom
                mx_ref[:, cc] = jnp.maximum(jnp.max(top, axis=0, keepdims=True),
                                            jnp.max(bottom, axis=0, keepdims=True))
                return
            s_ref[:nk, cc] = s
            mx_ref[:, cc] = jnp.max(s, axis=0, keepdims=True)

        def absorb_chunk(hd, c, j, bufs, kind):
            s_ref, mx_ref = bufs[hd]
            cc = slice(c * TQ, (c + 1) * TQ)
            nk = key_rows(kind, c)
            m_old = m_refs[hd][:, cc]
            m_new = jnp.maximum(m_old, mx_ref[:, cc])
            p = jnp.exp2(s_ref[:nk, cc] - m_new).astype(BF16)
            alpha = jnp.exp2(m_old - m_new)
            acc_refs[hd][:, cc] = alpha * acc_refs[hd][:, cc] + jnp.dot(
                vt_ref[j, hd * VT_ROWS:(hd + 1) * VT_ROWS, :nk], p, preferred_element_type=F32)
            m_refs[hd][:, cc] = m_new

        def fused(score=None, absorb=None):
            for hd in heads:
                for c in range(2):
                    if score is not None:
                        scores_chunk(hd, c, *score)
                    if absorb is not None:
                        absorb_chunk(hd, c, *absorb)

        def run(j, nsteps, pending, free):
            for step in range(nsteps):
                fused(score=(j - step - 1, free, PLAIN, i), absorb=(j - step, pending, PLAIN))
                pending, free = free, pending
            return pending

        def finish(pending, kind=PLAIN):
            nxt = jnp.minimum(i + 1, n_tiles - 1)
            fused(score=(nxt, buf_c, DIAGONAL, nxt), absorb=(0, pending, kind))

        @pl.when(i == 0)
        def _():
            reset()
            fused(score=(0, buf_a, DIAGONAL, i))
            finish(buf_a, DIAGONAL)

        for t in range(1, N_TILES):
            @pl.when(i == t)
            def _():
                write_out(i - 1)
                reset()
                fused(score=(i - 1, buf_b, BELOW_DIAGONAL, i), absorb=(i, buf_c, DIAGONAL))
                finish(run(t - 1, t - 1, buf_b, buf_a))

        return carry

    lax.fori_loop(0, n_tiles, tile, 0)
    write_out(n_tiles - 1)


def _attn(qs, k, vt, bias, lq1, lk1, lq2, lk2, subln_g, lam_init, batch, layer):
    t = k.shape[0]
    nq = SEQ // TQ
    hps = HEADS_PER_STEP
    per_head_scratch = ([pltpu.VMEM((TQ, 2 * TQ), F32)] * 3 + [pltpu.VMEM((1, 2 * TQ), F32)] * 4
                        + [pltpu.VMEM((VT_ROWS, 2 * TQ), F32)])
    return pl.pallas_call(
        functools.partial(_attn_kernel, lam_init=lam_init),
        grid=(ATT_HEADS // hps, batch),
        in_specs=[_layer_spec((1, QK_DIM), layer, 2)] * 4 + [
            _layer_spec((V_DIM, 1), layer, 2),
            pl.BlockSpec((nq, hps * V_DIM, 2 * TQ), lambda h, b: (b, h, 0)),
            pl.BlockSpec((SEQ, hps * V_DIM), lambda h, b: (b, h)),
            pl.BlockSpec((nq, hps * VT_ROWS, TQ), lambda h, b: (b, h, 0)),
            pl.BlockSpec((hps, TQ, 2 * TQ), lambda h, b: (h, 0, 0), pipeline_mode=pl.Buffered(1)),
            pl.BlockSpec((hps, CORNER, CORNER), lambda h, b: (h, 0, 0), pipeline_mode=pl.Buffered(1))],
        out_specs=pl.BlockSpec((SEQ, hps * V_DIM), lambda h, b: (b, h)),
        out_shape=jax.ShapeDtypeStruct((t, ATT_WIDTH), BF16),
        scratch_shapes=per_head_scratch * hps,
        compiler_params=pltpu.CompilerParams(dimension_semantics=("arbitrary",) * 2,
                                             vmem_limit_bytes=VMEM_LIMIT),
        name="attn",
    )(lq1, lk1, lq2, lk2, subln_g, qs, k, vt, *bias)


def _out_ffn_kernel(x_ref, att_ref, pc_ref, wo_ref, gffn_ref, wg_ref, wu_ref, wd_ref,
                    gfin_ref, o_ref, *, final):
    tm = x_ref.shape[0]
    halves = [slice(r * (tm // 2), (r + 1) * (tm // 2)) for r in range(2)]
    x1 = [x_ref[rows, :] + (jnp.dot(att_ref[rows, :], wo_ref[:ATT_WIDTH], preferred_element_type=F32)
                            + jnp.dot(pc_ref[rows, :], wo_ref[ATT_WIDTH:], preferred_element_type=F32))
          for rows in halves]
    h = [_rms(v, gffn_ref[...], EPS).astype(BF16) for v in x1]
    a = []
    for v in h:
        gate = jnp.dot(v, wg_ref[...], preferred_element_type=F32)
        up = jnp.dot(v, wu_ref[...], preferred_element_type=F32)
        a.append((jax.nn.silu(gate) * up).astype(BF16))
    for rows, v1, va in zip(halves, x1, a):
        x2 = v1 + jnp.dot(va, wd_ref[...], preferred_element_type=F32)
        o_ref[rows, :] = _rms(x2, gfin_ref[...], EPS) if final else x2


def _out_ffn(x2d, att, pc, wo, gffn, wg, wu, wd, gfin, layer, final):
    t = x2d.shape[0]
    tm = TM_FFN
    const = lambda shape: _layer_spec(shape, layer, 1)
    row = lambda width: pl.BlockSpec((tm, width), lambda i: (i, 0))
    return pl.pallas_call(
        functools.partial(_out_ffn_kernel, final=final),
        grid=(t // tm,),
        in_specs=[row(D_MODEL), row(ATT_WIDTH), row(ATT_WIDTH),
                  const((D_MODEL, D_MODEL)), const((1, D_MODEL)),
                  const((D_MODEL, D_FF)), const((D_MODEL, D_FF)), const((D_FF, D_MODEL)),
                  pl.BlockSpec((1, D_MODEL), lambda i: (0, 0))],
        out_specs=row(D_MODEL),
        out_shape=jax.ShapeDtypeStruct((t, D_MODEL), F32),
        compiler_params=pltpu.CompilerParams(dimension_semantics=("arbitrary",),
                                             vmem_limit_bytes=VMEM_LIMIT),
        name="out_ffn",
    )(x2d, att, pc, wo, gffn, wg, wu, wd, gfin)


def _block_diag(w_pool):
    depth, groups = w_pool.shape[:2]
    out = jnp.zeros((depth, POOL_WIDTH, POOL_WIDTH), w_pool.dtype)
    for g in range(groups):
        out = out.at[:, g * POOL_GDIM:(g + 1) * POOL_GDIM, g * POOL_GDIM:(g + 1) * POOL_GDIM].set(w_pool[:, g])
    return out


def kernel(x, g_mix, w_in, lambda_q1, lambda_k1, lambda_q2, lambda_k2, subln_g, rel_bias,
           w_pool, pool_scale, conv_w, w_o, g_ffn, w_gate, w_up, w_down, g_final):
    batch, seq, d = x.shape
    assert (seq, d) == (SEQ, D_MODEL)
    depth = w_in.shape[0]
    xf = x.reshape(batch * seq, d)
    bias = _bias_tiles(rel_bias.astype(F32))
    w_in_b, w_o_b, w_gate_b, w_up_b, w_down_b = (w.astype(BF16) for w in (w_in, w_o, w_gate, w_up, w_down))
    wbd = _block_diag(w_pool).astype(BF16)
    as_rows = lambda v: v[:, None, :]
    for l in range(depth):
        lam_init = 0.8 - 0.6 * math.exp(-0.3 * l)
        qs, k, vt, pc = _mix_in(xf, as_rows(g_mix), w_in_b, wbd, as_rows(pool_scale), conv_w, l)
        att = _attn(qs, k, vt, bias, as_rows(lambda_q1), as_rows(lambda_k1), as_rows(lambda_q2),
                    as_rows(lambda_k2), subln_g[:, :, None], lam_init, batch, l)
        xf = _out_ffn(xf, att, pc, w_o_b, as_rows(g_ffn), w_gate_b, w_up_b, w_down_b,
                      g_final[None], l, final=(l == depth - 1))
    return xf.reshape(batch, seq, d)
```

```python
import functools
import math

import numpy as np
import jax
import jax.numpy as jnp
from jax import lax
from jax.experimental import pallas as pl
from jax.experimental.pallas import tpu as pltpu

F32 = jnp.float32
BF16 = jnp.bfloat16

D_MODEL = 1024
SEQ = 4096
ATT_HEADS = 4
QK_DIM = 64
V_DIM = 2 * QK_DIM
ATT_WIDTH = ATT_HEADS * V_DIM
BF16_ROWS = 16
MXU_TILE = 256
VT_ROWS = V_DIM + BF16_ROWS
POOL_WINDOWS = (2, 4, 8, 16)
POOL_WIDTH = 256
POOL_GDIM = 64
CONV_WIDTH = 256
CONV_K = 3
IN_COLS = 3 * ATT_WIDTH + POOL_WIDTH + 3 * CONV_WIDTH
D_FF = 2816
NUM_BUCKETS = 32
MAX_EXACT = 16
MAX_DISTANCE = 128
EPS = 1e-6
SUBLN_EPS = 1e-5
NEG = -1e30
LOG2E = math.log2(math.e)

HALO = 16
TQ = 512
HALF = TQ // 2
CORNER = 128
PLAIN, BELOW_DIAGONAL, DIAGONAL = range(3)
HEADS_PER_STEP = 2
N_TILES = SEQ // TQ
TM_IN = TQ
TM_FFN = 512
VMEM_LIMIT = 56 * 1024 * 1024


def _bucket_thresholds():
    n = np.arange(MAX_DISTANCE)
    nf = np.maximum(n, 1).astype(np.float32)
    large = MAX_EXACT + (np.log(nf / MAX_EXACT) / math.log(MAX_DISTANCE / MAX_EXACT)
                         * (NUM_BUCKETS - MAX_EXACT)).astype(np.int32)
    bucket = np.where(n < MAX_EXACT, n, np.minimum(large, NUM_BUCKETS - 1))
    assert bucket[-1] == NUM_BUCKETS - 1 and np.all(np.diff(bucket) >= 0)
    return [int(np.argmax(bucket >= b)) for b in range(NUM_BUCKETS)]


_BUCKET_THR = _bucket_thresholds()


def _rms(x, g, eps):
    return (x * lax.rsqrt(jnp.mean(x * x, axis=-1, keepdims=True) + eps)) * g


def _bias_value(relb_ref, h, dist):
    val = jnp.full(dist.shape, relb_ref[0, h], F32)
    for b in range(1, NUM_BUCKETS):
        val = jnp.where(dist >= _BUCKET_THR[b], relb_ref[b, h], val)
    return jnp.where(dist >= 0, (val - relb_ref[NUM_BUCKETS - 1, h]) * LOG2E, NEG)


def _bias_kernel(relb_ref, diag_ref, corner_ref):
    h = pl.program_id(0)
    kpos = lax.broadcasted_iota(jnp.int32, (TQ, TQ), 0)
    qpos = lax.broadcasted_iota(jnp.int32, (TQ, TQ), 1)
    val = _bias_value(relb_ref, h, qpos - kpos)
    for chunk in range(4):
        diag_ref[:, chunk * HALF:(chunk + 1) * HALF] = val[:, (chunk // 2) * HALF:(chunk // 2 + 1) * HALF]
    kpos = lax.broadcasted_iota(jnp.int32, (CORNER, CORNER), 0) + (TQ - CORNER)
    qpos = lax.broadcasted_iota(jnp.int32, (CORNER, CORNER), 1)
    corner_ref[...] = _bias_value(relb_ref, h, TQ + qpos - kpos)


def _bias_tiles(rel_bias):
    assert CORNER >= MAX_DISTANCE and _BUCKET_THR[NUM_BUCKETS - 1] <= MAX_DISTANCE
    return pl.pallas_call(
        _bias_kernel,
        grid=(ATT_HEADS,),
        in_specs=[pl.BlockSpec(memory_space=pltpu.SMEM)],
        out_specs=[pl.BlockSpec((None, TQ, 2 * TQ), lambda h: (h, 0, 0)),
                   pl.BlockSpec((None, CORNER, CORNER), lambda h: (h, 0, 0))],
        out_shape=[jax.ShapeDtypeStruct((ATT_HEADS, TQ, 2 * TQ), F32),
                   jax.ShapeDtypeStruct((ATT_HEADS, CORNER, CORNER), F32)],
        name="bias_tiles",
    )(rel_bias)


def _mix_in_kernel(x_ref, g_ref, w_ref, wbd_ref, pscale_ref, convw_ref,
                   qs_ref, k_ref, vt_ref, pc_ref, carry_ref):
    tm = x_ref.shape[0]
    tiles_per_seq = SEQ // tm
    seq_tile = pl.program_id(0) % tiles_per_seq

    @pl.when(seq_tile == 0)
    def _():
        carry_ref[...] = jnp.zeros_like(carry_ref)

    h = _rms(x_ref[...], g_ref[...], EPS).astype(BF16)
    o = 3 * ATT_WIDTH
    proj = jnp.dot(h, w_ref[:, o:], preferred_element_type=F32)
    p = proj[:, :POOL_WIDTH]
    gb = proj[:, POOL_WIDTH:POOL_WIDTH + CONV_WIDTH]
    gc = proj[:, POOL_WIDTH + CONV_WIDTH:POOL_WIDTH + 2 * CONV_WIDTH]
    hin = proj[:, POOL_WIDTH + 2 * CONV_WIDTH:]
    u = gc * hin

    qkv = jnp.dot(h, w_ref[:, :o], preferred_element_type=F32)
    qt = (qkv[:, 0:ATT_WIDTH] * (QK_DIM ** -0.5 * LOG2E)).T.astype(BF16)
    first_map = lax.broadcasted_iota(jnp.int32, qt.shape, 0) % V_DIM < QK_DIM
    zero = jnp.zeros_like(qt)
    maps = (jnp.where(first_map, qt, zero), jnp.where(first_map, zero, qt))
    for chunk in range(4):
        qs_ref[:, chunk * HALF:(chunk + 1) * HALF] = maps[chunk % 2][:, (chunk // 2) * HALF:(chunk // 2 + 1) * HALF]
    k_ref[...] = qkv[:, ATT_WIDTH:2 * ATT_WIDTH].astype(BF16)
    vt = qkv[:, 2 * ATT_WIDTH:].T.astype(BF16)
    ones = jnp.ones((BF16_ROWS, tm), BF16)
    for hd in range(ATT_HEADS):
        vt_ref[hd * VT_ROWS:hd * VT_ROWS + V_DIM] = vt[hd * V_DIM:(hd + 1) * V_DIM]
        vt_ref[hd * VT_ROWS + V_DIM:(hd + 1) * VT_ROWS] = ones

    p_ext = jnp.concatenate([carry_ref[:, :POOL_WIDTH], p], axis=0)
    u_ext = jnp.concatenate([carry_ref[:, POOL_WIDTH:], u], axis=0)
    carry_ref[:, :POOL_WIDTH] = p[tm - HALO:]
    carry_ref[:, POOL_WIDTH:] = u[tm - HALO:]

    a2 = p_ext + pltpu.roll(p_ext, 1, axis=0)
    a4 = a2 + pltpu.roll(a2, 2, axis=0)
    a4_hi = a4[:, 128:]
    a8 = a4_hi + pltpu.roll(a4_hi, 4, axis=0)
    a16 = a8 + pltpu.roll(a8, 8, axis=0)
    lane = lax.broadcasted_iota(jnp.int32, (tm, 128), 1)
    low = lane < POOL_GDIM
    sums = jnp.concatenate([jnp.where(low, a2[HALO:, :128], a4[HALO:, :128]),
                            jnp.where(low, a8[HALO:], a16[HALO:])], axis=1)
    col = lax.broadcasted_iota(jnp.int32, (tm, POOL_WIDTH), 1)
    win = jnp.left_shift(2, col // POOL_GDIM)
    t1 = seq_tile * tm + lax.broadcasted_iota(jnp.int32, (tm, POOL_WIDTH), 0) + 1
    pooled = sums / jnp.minimum(t1, win).astype(F32) - p
    mixed = jnp.dot(pooled.astype(BF16), wbd_ref[...], preferred_element_type=F32) * pscale_ref[...]
    pc_ref[:, :POOL_WIDTH] = mixed.astype(BF16)

    y = (convw_ref[0:1, :] * pltpu.roll(u_ext, 2, axis=0)[HALO:]
         + convw_ref[1:2, :] * pltpu.roll(u_ext, 1, axis=0)[HALO:]
         + convw_ref[2:3, :] * u)
    pc_ref[:, POOL_WIDTH:] = (gb * y).astype(BF16)


def _layer_spec(shape, layer, grid_rank):
    index_map = {1: lambda i: (layer, 0, 0), 2: lambda h, b: (layer, 0, 0)}[grid_rank]
    return pl.BlockSpec((None,) + shape, index_map, pipeline_mode=pl.Buffered(1))


def _mix_in(x2d, g, w_in, wbd, pscale, convw, layer):
    t = x2d.shape[0]
    tm = TM_IN
    const = lambda shape: _layer_spec(shape, layer, 1)
    row = lambda width: pl.BlockSpec((tm, width), lambda i: (i, 0))
    return pl.pallas_call(
        _mix_in_kernel,
        grid=(t // tm,),
        in_specs=[row(D_MODEL), const((1, D_MODEL)), const((D_MODEL, IN_COLS)),
                  const((POOL_WIDTH, POOL_WIDTH)), const((1, POOL_WIDTH)), const((CONV_K, CONV_WIDTH))],
        out_specs=[pl.BlockSpec((None, ATT_WIDTH, 2 * tm), lambda i: (i, 0, 0)), row(ATT_WIDTH),
                   pl.BlockSpec((None, ATT_HEADS * VT_ROWS, tm), lambda i: (i, 0, 0)), row(ATT_WIDTH)],
        out_shape=[jax.ShapeDtypeStruct((t // tm, ATT_WIDTH, 2 * tm), BF16),
                   jax.ShapeDtypeStruct((t, ATT_WIDTH), BF16),
                   jax.ShapeDtypeStruct((t // tm, ATT_HEADS * VT_ROWS, tm), BF16),
                   jax.ShapeDtypeStruct((t, ATT_WIDTH), BF16)],
        scratch_shapes=[pltpu.VMEM((HALO, POOL_WIDTH + CONV_WIDTH), F32)],
        compiler_params=pltpu.CompilerParams(dimension_semantics=("arbitrary",),
                                             vmem_limit_bytes=VMEM_LIMIT),
        name="mix_in",
    )(x2d, g, w_in, wbd, pscale, convw)


def _attn_kernel(lq1_ref, lk1_ref, lq2_ref, lk2_ref, g_ref, qs_ref, k_ref, vt_ref, bias_ref, corner_ref, o_ref,
                 *scratch, lam_init):
    n_tiles = qs_ref.shape[0]
    assert n_tiles == N_TILES
    heads = range(HEADS_PER_STEP)
    per_head = [scratch[hd * 8:(hd + 1) * 8] for hd in heads]
    buf_a = [(sc[0], sc[3]) for sc in per_head]
    buf_b = [(sc[1], sc[4]) for sc in per_head]
    buf_c = [(sc[2], sc[5]) for sc in per_head]
    m_refs = [sc[6] for sc in per_head]
    acc_refs = [sc[7] for sc in per_head]
    lam = (jnp.exp(jnp.sum(lq1_ref[...] * lk1_ref[...], axis=-1, keepdims=True))
           - jnp.exp(jnp.sum(lq2_ref[...] * lk2_ref[...], axis=-1, keepdims=True)) + lam_init)

    def write_out(tile_index):
        for hd in heads:
            out = acc_refs[hd][:V_DIM] / acc_refs[hd][V_DIM:V_DIM + 1]
            att = jnp.concatenate([out[:, 0:HALF] - lam * out[:, HALF:TQ],
                                   out[:, TQ:TQ + HALF] - lam * out[:, TQ + HALF:]], axis=1)
            y = (att * lax.rsqrt(jnp.mean(att * att, axis=0, keepdims=True) + SUBLN_EPS)) * g_ref[...]
            o_ref[pl.ds(pl.multiple_of(tile_index * TQ, TQ), TQ), hd * V_DIM:(hd + 1) * V_DIM] = (
                (y * (1.0 - lam_init)).T.astype(BF16))

    def tile(i, carry):
        def reset():
            for hd in heads:
                m_refs[hd][...] = jnp.full_like(m_refs[hd], NEG)
                acc_refs[hd][...] = jnp.zeros_like(acc_refs[hd])

        def key_rows(kind, c):
            return HALF if (kind == DIAGONAL and c == 0) else TQ

        def scores_chunk(hd, c, j, bufs, kind, qtile):
            s_ref, mx_ref = bufs[hd]
            cols = slice(hd * V_DIM, (hd + 1) * V_DIM)
            cc = slice(c * TQ, (c + 1) * TQ)
            nk = key_rows(kind, c)
            kb = k_ref[pl.ds(pl.multiple_of(j * TQ, TQ), nk), cols]
            s = jnp.dot(kb, qs_ref[qtile, cols, cc], preferred_element_type=F32)
            if kind == DIAGONAL:
                s = s + bias_ref[hd, :nk, cc]
            if kind == BELOW_DIAGONAL and c == 0:
                top, bottom = s[:TQ - CORNER], s[TQ - CORNER:]
                corner = corner_ref[hd]
                bottom = jnp.concatenate(
                    [bottom[:, :CORNER] + corner, bottom[:, CORNER:HALF],
                     bottom[:, HALF:HALF + CORNER] + corner, bottom[:, HALF + CORNER:]], axis=1)
                s_ref[:TQ - CORNER, cc] = top
                s_ref[TQ - CORNER:, cc] = bottom
                mx_ref[:, cc] = jnp.maximum(jnp.max(top, axis=0, keepdims=True),
                                            jnp.max(bottom, axis=0, keepdims=True))
                return
            s_ref[:nk, cc] = s
            mx_ref[:, cc] = jnp.max(s, axis=0, keepdims=True)

        def absorb_chunk(hd, c, j, bufs, kind):
            s_ref, mx_ref = bufs[hd]
            cc = slice(c * TQ, (c + 1) * TQ)
            nk = key_rows(kind, c)
            m_old = m_refs[hd][:, cc]
            m_new = jnp.maximum(m_old, mx_ref[:, cc])
            p = jnp.exp2(s_ref[:nk, cc] - m_new).astype(BF16)
            alpha = jnp.exp2(m_old - m_new)
            acc_refs[hd][:, cc] = alpha * acc_refs[hd][:, cc] + jnp.dot(
                vt_ref[j, hd * VT_ROWS:(hd + 1) * VT_ROWS, :nk], p, preferred_element_type=F32)
            m_refs[hd][:, cc] = m_new

        def fused(score=None, absorb=None):
            for hd in heads:
                for c in range(2):
                    if score is not None:
                        scores_chunk(hd, c, *score)
                    if absorb is not None:
                        absorb_chunk(hd, c, *absorb)

        def run(j, nsteps, pending, free):
            for step in range(nsteps):
                fused(score=(j - step - 1, free, PLAIN, i), absorb=(j - step, pending, PLAIN))
                pending, free = free, pending
            return pending

        def finish(pending, kind=PLAIN):
            nxt = jnp.minimum(i + 1, n_tiles - 1)
            fused(score=(nxt, buf_c, DIAGONAL, nxt), absorb=(0, pending, kind))

        @pl.when(i == 0)
        def _():
            reset()
            fused(score=(0, buf_a, DIAGONAL, i))
            finish(buf_a, DIAGONAL)

        for t in range(1, N_TILES):
            @pl.when(i == t)
            def _():
                write_out(i - 1)
                reset()
                fused(score=(i - 1, buf_b, BELOW_DIAGONAL, i), absorb=(i, buf_c, DIAGONAL))
                finish(run(t - 1, t - 1, buf_b, buf_a))

        return carry

    lax.fori_loop(0, n_tiles, tile, 0)
    write_out(n_tiles - 1)


def _attn(qs, k, vt, bias, lq1, lk1, lq2, lk2, subln_g, lam_init, batch, layer):
    t = k.shape[0]
    nq = SEQ // TQ
    hps = HEADS_PER_STEP
    per_head_scratch = ([pltpu.VMEM((TQ, 2 * TQ), F32)] * 3 + [pltpu.VMEM((1, 2 * TQ), F32)] * 4
                        + [pltpu.VMEM((VT_ROWS, 2 * TQ), F32)])
    return pl.pallas_call(
        functools.partial(_attn_kernel, lam_init=lam_init),
        grid=(ATT_HEADS // hps, batch),
        in_specs=[_layer_spec((1, QK_DIM), layer, 2)] * 4 + [
            _layer_spec((V_DIM, 1), layer, 2),
            pl.BlockSpec((nq, hps * V_DIM, 2 * TQ), lambda h, b: (b, h, 0)),
            pl.BlockSpec((SEQ, hps * V_DIM), lambda h, b: (b, h)),
            pl.BlockSpec((nq, hps * VT_ROWS, TQ), lambda h, b: (b, h, 0)),
            pl.BlockSpec((hps, TQ, 2 * TQ), lambda h, b: (h, 0, 0), pipeline_mode=pl.Buffered(1)),
            pl.BlockSpec((hps, CORNER, CORNER), lambda h, b: (h, 0, 0), pipeline_mode=pl.Buffered(1))],
        out_specs=pl.BlockSpec((SEQ, hps * V_DIM), lambda h, b: (b, h)),
        out_shape=jax.ShapeDtypeStruct((t, ATT_WIDTH), BF16),
        scratch_shapes=per_head_scratch * hps,
        compiler_params=pltpu.CompilerParams(dimension_semantics=("arbitrary",) * 2,
                                             vmem_limit_bytes=VMEM_LIMIT),
        name="attn",
    )(lq1, lk1, lq2, lk2, subln_g, qs, k, vt, *bias)


def _out_ffn_kernel(x_ref, att_ref, pc_ref, wo_ref, gffn_ref, wg_ref, wu_ref, wd_ref,
                    gfin_ref, o_ref, *, final):
    tm = x_ref.shape[0]
    halves = [slice(r * (tm // 2), (r + 1) * (tm // 2)) for r in range(2)]
    x1 = [x_ref[rows, :] + (jnp.dot(att_ref[rows, :], wo_ref[:ATT_WIDTH], preferred_element_type=F32)
                            + jnp.dot(pc_ref[rows, :], wo_ref[ATT_WIDTH:], preferred_element_type=F32))
          for rows in halves]
    h = [_rms(v, gffn_ref[...], EPS).astype(BF16) for v in x1]
    split = (D_FF // 2 + MXU_TILE - 1) // MXU_TILE * MXU_TILE
    chunks = (slice(0, split), slice(split, D_FF))
    ff = []
    for v in h:
        parts = []
        for cols in chunks:
            gate = jnp.dot(v, wg_ref[:, cols], preferred_element_type=F32)
            up = jnp.dot(v, wu_ref[:, cols], preferred_element_type=F32)
            parts.append((jax.nn.silu(gate) * up).astype(BF16))
        ff.append(sum(jnp.dot(a, wd_ref[cols, :], preferred_element_type=F32) for a, cols in zip(parts, chunks)))
    for rows, v1, f in zip(halves, x1, ff):
        x2 = v1 + f
        o_ref[rows, :] = _rms(x2, gfin_ref[...], EPS) if final else x2


def _out_ffn(x2d, att, pc, wo, gffn, wg, wu, wd, gfin, layer, final):
    t = x2d.shape[0]
    tm = TM_FFN
    const = lambda shape: _layer_spec(shape, layer, 1)
    row = lambda width: pl.BlockSpec((tm, width), lambda i: (i, 0))
    return pl.pallas_call(
        functools.partial(_out_ffn_kernel, final=final),
        grid=(t // tm,),
        in_specs=[row(D_MODEL), row(ATT_WIDTH), row(ATT_WIDTH),
                  const((D_MODEL, D_MODEL)), const((1, D_MODEL)),
                  const((D_MODEL, D_FF)), const((D_MODEL, D_FF)), const((D_FF, D_MODEL)),
                  pl.BlockSpec((1, D_MODEL), lambda i: (0, 0))],
        out_specs=row(D_MODEL),
        out_shape=jax.ShapeDtypeStruct((t, D_MODEL), F32),
        compiler_params=pltpu.CompilerParams(dimension_semantics=("arbitrary",),
                                             vmem_limit_bytes=VMEM_LIMIT),
        name="out_ffn",
    )(x2d, att, pc, wo, gffn, wg, wu, wd, gfin)


def _block_diag(w_pool):
    depth, groups = w_pool.shape[:2]
    out = jnp.zeros((depth, POOL_WIDTH, POOL_WIDTH), w_pool.dtype)
    for g in range(groups):
        out = out.at[:, g * POOL_GDIM:(g + 1) * POOL_GDIM, g * POOL_GDIM:(g + 1) * POOL_GDIM].set(w_pool[:, g])
    return out


def kernel(x, g_mix, w_in, lambda_q1, lambda_k1, lambda_q2, lambda_k2, subln_g, rel_bias,
           w_pool, pool_scale, conv_w, w_o, g_ffn, w_gate, w_up, w_down, g_final):
    batch, seq, d = x.shape
    assert (seq, d) == (SEQ, D_MODEL)
    depth = w_in.shape[0]
    xf = x.reshape(batch * seq, d)
    bias = _bias_tiles(rel_bias.astype(F32))
    w_in_b, w_o_b, w_gate_b, w_up_b, w_down_b = (w.astype(BF16) for w in (w_in, w_o, w_gate, w_up, w_down))
    wbd = _block_diag(w_pool).astype(BF16)
    as_rows = lambda v: v[:, None, :]
    for l in range(depth):
        lam_init = 0.8 - 0.6 * math.exp(-0.3 * l)
        qs, k, vt, pc = _mix_in(xf, as_rows(g_mix), w_in_b, wbd, as_rows(pool_scale), conv_w, l)
        att = _attn(qs, k, vt, bias, as_rows(lambda_q1), as_rows(lambda_k1), as_rows(lambda_q2),
                    as_rows(lambda_k2), subln_g[:, :, None], lam_init, batch, l)
        xf = _out_ffn(xf, att, pc, w_o_b, as_rows(g_ffn), w_gate_b, w_up_b, w_down_b,
                      g_final[None], l, final=(l == depth - 1))
    return xf.reshape(batch, seq, d)
```
